```python
import math
import jax, jax.numpy as jnp
from jax import lax
import numpy as np

D_MODEL = 1024
BATCH = 16
SEQ = 2048
DEPTH = 1

ATTN_Q_HEADS = 8
ATTN_KV_HEADS = 2
ATTN_HEAD_DIM = 64
WINDOW = 128
REL_BUCKETS = 32
REL_MAX_DIST = 128
DN_HEADS = 4
DN_HEAD_DIM = 128
DN_CONV = 4
DN_CHUNK = 64
D_FF = 2816
FFN_CONV = 3
RMS_EPS = 1e-6
L2_EPS = 1e-6
N_MOD = 6
NEG_INF = -1e30

ATTN_Q_DIM = ATTN_Q_HEADS * ATTN_HEAD_DIM
ATTN_KV_DIM = ATTN_KV_HEADS * ATTN_HEAD_DIM
DN_DIM = DN_HEADS * DN_HEAD_DIM
IN_SPLIT_SIZES = (ATTN_Q_DIM, ATTN_KV_DIM, ATTN_KV_DIM, 3 * DN_DIM, DN_DIM, DN_HEADS, DN_HEADS, D_MODEL, D_MODEL)
IN_DIM = sum(IN_SPLIT_SIZES)

kernel_name = 'hybrid_swa_gdn_convffn_block'


def rms_norm(x, w):
    xf = x.astype(jnp.float32)
    y = xf * lax.rsqrt(jnp.mean(xf * xf, axis=-1, keepdims=True) + RMS_EPS)
    return (y * w.astype(jnp.float32)).astype(x.dtype)


def l2_normalize(x):
    return x * lax.rsqrt(jnp.sum(x * x, axis=-1, keepdims=True) + L2_EPS)


def causal_depthwise_conv(x, w):
    k, ch = w.shape
    return lax.conv_general_dilated(x, w[:, None, :].astype(x.dtype), window_strides=(1,), padding=[(k - 1, 0)], dimension_numbers=('NWC', 'WIO', 'NWC'), feature_group_count=ch)


def t5_causal_bucket(dist):
    dist = jnp.maximum(dist, 0)
    max_exact = REL_BUCKETS // 2
    scaled = jnp.log(jnp.maximum(dist, 1).astype(jnp.float32) / max_exact) / math.log(REL_MAX_DIST / max_exact)
    large = max_exact + (scaled * (REL_BUCKETS - max_exact)).astype(jnp.int32)
    large = jnp.minimum(large, REL_BUCKETS - 1)
    return jnp.where(dist < max_exact, dist, large)


def sliding_window_gqa(q, k, v, sinks, rel_bias):
    b, s, _, hd = q.shape
    grp = ATTN_Q_HEADS // ATTN_KV_HEADS
    nb = s // WINDOW
    qb = q.reshape(b, nb, WINDOW, ATTN_KV_HEADS, grp, hd)

    def band(t):
        tb = t.reshape(b, nb, WINDOW, ATTN_KV_HEADS, hd)
        prev = jnp.concatenate([jnp.zeros_like(tb[:, :1]), tb[:, :-1]], axis=1)
        return jnp.concatenate([prev, tb], axis=2)

    kb, vb = band(k), band(v)
    scores = jnp.einsum('bnqhgd,bnkhd->bnhgqk', qb, kb, preferred_element_type=jnp.float32) * (hd ** -0.5)
    qi = jnp.arange(WINDOW)[:, None]
    kj = jnp.arange(2 * WINDOW)[None, :]
    dist = WINDOW + qi - kj
    bias = rel_bias[t5_causal_bucket(dist)]
    bias = jnp.transpose(bias, (2, 0, 1)).reshape(ATTN_KV_HEADS, grp, WINDOW, 2 * WINDOW).astype(jnp.float32)
    in_band = (dist >= 0) & (dist < WINDOW)
    key_pos = jnp.arange(nb)[:, None, None] * WINDOW - WINDOW + kj[None]
    mask = in_band[None] & (key_pos >= 0)
    scores = jnp.where(mask[None, :, None, None], scores + bias, NEG_INF)
    sink = sinks.astype(jnp.float32).reshape(1, 1, ATTN_KV_HEADS, grp, 1, 1)
    m = jnp.maximum(jnp.max(scores, axis=-1, keepdims=True), sink)
    p = jnp.exp(scores - m)
    probs = p / (jnp.sum(p, axis=-1, keepdims=True) + jnp.exp(sink - m))
    out = jnp.einsum('bnhgqk,bnkhd->bnqhgd', probs.astype(v.dtype), vb)
    return out.reshape(b, s, ATTN_Q_DIM)


def chunk_gated_delta_rule(q, k, v, g, beta):
    b, s, nh, dk = q.shape
    dv = v.shape[-1]
    n = s // DN_CHUNK

    def chunks(t):
        return jnp.swapaxes(t, 1, 2).reshape(b, nh, n, DN_CHUNK, t.shape[-1])

    q = chunks(q) * (dk ** -0.5)
    k = chunks(k)
    v = chunks(v)
    gc = jnp.cumsum(jnp.swapaxes(g, 1, 2).reshape(b, nh, n, DN_CHUNK), axis=-1)
    bt = jnp.swapaxes(beta, 1, 2).reshape(b, nh, n, DN_CHUNK)[..., None]
    k_beta = k * bt
    v_beta = v * bt
    incl = jnp.tril(jnp.ones((DN_CHUNK, DN_CHUNK), dtype=bool))
    strict = jnp.tril(jnp.ones((DN_CHUNK, DN_CHUNK), dtype=bool), k=-1)
    diff = gc[..., :, None] - gc[..., None, :]
    decay = jnp.where(incl, jnp.exp(jnp.where(incl, diff, 0.0)), 0.0)
    eg = jnp.exp(gc)[..., None]
    lower = jnp.where(strict, jnp.einsum('bhncd,bhnmd->bhncm', k_beta, k) * decay, 0.0)
    rhs = jnp.concatenate([v_beta, k_beta * eg], axis=-1)
    eye = jnp.eye(DN_CHUNK, dtype=jnp.float32)
    sol = lax.linalg.triangular_solve(eye + lower, rhs, left_side=True, lower=True)
    u, w = sol[..., :dv], sol[..., dv:]
    intra = jnp.where(incl, jnp.einsum('bhncd,bhnmd->bhncm', q, k) * decay, 0.0)
    q_dec = q * eg
    k_tail = k * jnp.exp(gc[..., -1:] - gc)[..., None]
    g_last = jnp.exp(gc[..., -1])

    def step(state, inp):
        q_i, k_i, u_i, w_i, a_i, gl_i = inp
        v_new = u_i - jnp.einsum('bhcd,bhde->bhce', w_i, state)
        o_i = jnp.einsum('bhcd,bhde->bhce', q_i, state) + jnp.einsum('bhcm,bhme->bhce', a_i, v_new)
        state = state * gl_i[..., None, None] + jnp.einsum('bhcd,bhce->bhde', k_i, v_new)
        return state, o_i

    xs = tuple(jnp.moveaxis(t, 2, 0) for t in (q_dec, k_tail, u, w, intra, g_last))
    state0 = jnp.zeros((b, nh, dk, dv), jnp.float32)
    _, o = lax.scan(step, state0, xs)
    o = jnp.moveaxis(o, 0, 2).reshape(b, nh, s, dv)
    return jnp.swapaxes(o, 1, 2)


def gated_deltanet(qkv, beta_logits, a_logits, z, conv_w, a_log, dt_bias, norm_w):
    b, s, _ = qkv.shape
    qkv = jax.nn.silu(causal_depthwise_conv(qkv, conv_w)).astype(jnp.float32)
    q, k, v = jnp.split(qkv, 3, axis=-1)
    q = l2_normalize(q.reshape(b, s, DN_HEADS, DN_HEAD_DIM))
    k = l2_normalize(k.reshape(b, s, DN_HEADS, DN_HEAD_DIM))
    v = v.reshape(b, s, DN_HEADS, DN_HEAD_DIM)
    beta = jax.nn.sigmoid(beta_logits.astype(jnp.float32))
    g = -jnp.exp(a_log.astype(jnp.float32)) * jax.nn.softplus(a_logits.astype(jnp.float32) + dt_bias.astype(jnp.float32))
    o = chunk_gated_delta_rule(q, k, v, g, beta)
    zf = z.astype(jnp.float32).reshape(b, s, DN_HEADS, DN_HEAD_DIM)
    o = rms_norm(o, norm_w) * jax.nn.silu(zf)
    return o.reshape(b, s, DN_DIM).astype(z.dtype)


def hybrid_mixer(h, w_in, dn_conv_w, dn_a_log, dn_dt_bias, dn_norm_w, attn_sinks, rel_bias, w_attn_branch, w_dn_branch, w_out):
    b, s, _ = h.shape
    proj = h @ w_in
    splits = np.cumsum(IN_SPLIT_SIZES)[:-1].tolist()
    aq, ak, av, dqkv, dz, dbeta, da, gate_a, gate_d = jnp.split(proj, splits, axis=-1)
    y_attn = sliding_window_gqa(aq.reshape(b, s, ATTN_Q_HEADS, ATTN_HEAD_DIM), ak.reshape(b, s, ATTN_KV_HEADS, ATTN_HEAD_DIM), av.reshape(b, s, ATTN_KV_HEADS, ATTN_HEAD_DIM), attn_sinks, rel_bias)
    y_dn = gated_deltanet(dqkv, dbeta, da, dz, dn_conv_w, dn_a_log, dn_dt_bias, dn_norm_w)
    merged = jax.nn.sigmoid(gate_a) * (y_attn @ w_attn_branch) + jax.nn.sigmoid(gate_d) * (y_dn @ w_dn_branch)
    return merged @ w_out


def conv_ffn(h, w_up, conv_w, w_down):
    u = causal_depthwise_conv(h @ w_up, conv_w)
    gate, val = jnp.split(u, 2, axis=-1)
    return (jax.nn.gelu(gate, approximate=True) * val) @ w_down


def setup_inputs(seed: int = 0) -> dict:
    key = jax.random.key(seed)
    ks = jax.random.split(key, 22)
    f32 = jnp.float32
    L = DEPTH

    def normal(k, shape, scale):
        return jax.random.normal(k, shape, f32) * scale

    def gain(k, shape):
        return 1.0 + 0.05 * jax.random.normal(k, shape, f32)

    dt = jnp.exp(jax.random.uniform(ks[11], (L, DN_HEADS), f32, math.log(1e-3), math.log(1e-1)))
    return {
        'x': normal(ks[0], (BATCH, SEQ, D_MODEL), 1.0),
        'c': normal(ks[1], (BATCH, D_MODEL), 1.0),
        'ada_w': normal(ks[2], (L, D_MODEL, N_MOD * D_MODEL), 0.5 * D_MODEL ** -0.5),
        'ada_b': normal(ks[3], (L, N_MOD * D_MODEL), 0.02),
        'norm_mix_pre': gain(ks[4], (L, D_MODEL)),
        'norm_mix_post': gain(ks[5], (L, D_MODEL)),
        'norm_ffn_pre': gain(ks[6], (L, D_MODEL)),
        'norm_ffn_post': gain(ks[7], (L, D_MODEL)),
        'w_in': normal(ks[8], (L, D_MODEL, IN_DIM), D_MODEL ** -0.5),
        'dn_conv_w': normal(ks[9], (L, DN_CONV, 3 * DN_DIM), DN_CONV ** -0.5),
        'dn_a_log': jnp.log(jax.random.uniform(ks[10], (L, DN_HEADS), f32, 1.0, 16.0)),
        'dn_dt_bias': dt + jnp.log(-jnp.expm1(-dt)),
        'dn_norm_w': gain(ks[12], (L, DN_HEAD_DIM)),
        'attn_sinks': normal(ks[13], (L, ATTN_Q_HEADS), 1.0),
        'rel_bias': normal(ks[14], (REL_BUCKETS, ATTN_Q_HEADS), 0.5),
        'w_attn_branch': normal(ks[15], (L, ATTN_Q_DIM, D_MODEL), ATTN_Q_DIM ** -0.5),
        'w_dn_branch': normal(ks[16], (L, DN_DIM, D_MODEL), DN_DIM ** -0.5),
        'w_out': normal(ks[17], (L, D_MODEL, D_MODEL), D_MODEL ** -0.5),
        'ffn_w_up': normal(ks[18], (L, D_MODEL, 2 * D_FF), D_MODEL ** -0.5),
        'ffn_conv_w': normal(ks[19], (L, FFN_CONV, 2 * D_FF), FFN_CONV ** -0.5),
        'ffn_w_down': normal(ks[20], (L, D_FF, D_MODEL), D_FF ** -0.5),
    }


def reference(x, c, ada_w, ada_b, norm_mix_pre, norm_mix_post, norm_ffn_pre, norm_ffn_post, w_in, dn_conv_w, dn_a_log, dn_dt_bias, dn_norm_w, attn_sinks, rel_bias, w_attn_branch, w_dn_branch, w_out, ffn_w_up, ffn_conv_w, ffn_w_down):
    h = x
    c_act = jax.nn.silu(c)
    for l in range(DEPTH):
        mod = c_act @ ada_w[l] + ada_b[l]
        sh1, sc1, g1, sh2, sc2, g2 = [m[:, None, :] for m in jnp.split(mod, N_MOD, axis=-1)]
        u = rms_norm(h, norm_mix_pre[l]) * (1.0 + sc1) + sh1
        y = hybrid_mixer(u, w_in[l], dn_conv_w[l], dn_a_log[l], dn_dt_bias[l], dn_norm_w[l], attn_sinks[l], rel_bias, w_attn_branch[l], w_dn_branch[l], w_out[l])
        h = h + g1 * rms_norm(y, norm_mix_post[l])
        u = rms_norm(h, norm_ffn_pre[l]) * (1.0 + sc2) + sh2
        y = conv_ffn(u, ffn_w_up[l], ffn_conv_w[l], ffn_w_down[l])
        h = h + g2 * rms_norm(y, norm_ffn_post[l])
    return h
```

```python
import functools
import math

import numpy as np
import jax
import jax.numpy as jnp
from jax import lax
from jax.experimental import pallas as pl
from jax.experimental.pallas import tpu as pltpu

ATTN_Q_HEADS = 8
ATTN_KV_HEADS = 2
ATTN_HEAD_DIM = 64
WINDOW = 128
REL_BUCKETS = 32
REL_MAX_DIST = 128
DN_HEADS = 4
DN_HEAD_DIM = 128
DN_CONV = 4
DN_CHUNK = 64
D_FF = 2816
FFN_CONV = 3
RMS_EPS = 1e-6
L2_EPS = 1e-6
N_MOD = 6
NEG_INF = -1e30

ATTN_Q_DIM = ATTN_Q_HEADS * ATTN_HEAD_DIM
ATTN_KV_DIM = ATTN_KV_HEADS * ATTN_HEAD_DIM
DN_DIM = DN_HEADS * DN_HEAD_DIM
ATTN_GROUP = ATTN_Q_HEADS // ATTN_KV_HEADS

V7X_LANES = 128
V7X_SUBLANES = 8
V7X_VMEM_BYTES = 64 * 1024 * 1024

F32 = jnp.float32
BF16 = jnp.bfloat16

ROW_TILE = 512
DN_ROWS = 256
FF_CHUNK = 256
BD_PAD = V7X_LANES


def _vmem_limit(nbytes):
    return int(min(V7X_VMEM_BYTES, nbytes + nbytes // 2 + (8 << 20)))


def _resident(shape):
    nd = len(shape)
    return pl.BlockSpec(shape, lambda *_: (0,) * nd, pipeline_mode=pl.Buffered(1))


def _dot(a, b):
    return jnp.dot(a, b, preferred_element_type=F32)


def _dot_nt(a, b):
    return lax.dot_general(a, b, (((1,), (1,)), ((), ())), preferred_element_type=F32)


def _dot_tn(a, b):
    return lax.dot_general(a, b, (((0,), (0,)), ((), ())), preferred_element_type=F32)


def _rms(xf, w):
    return xf * lax.rsqrt(jnp.mean(xf * xf, axis=-1, keepdims=True) + RMS_EPS) * w


def _sigmoid(x):
    return 1.0 / (1.0 + jnp.exp(-x))


def _silu(x):
    return x * _sigmoid(x)


def _mod_body(c_ref, w_ref, b_ref, o_ref):
    ca = _silu(c_ref[...])
    o_ref[...] = _dot(ca.astype(BF16), w_ref[...].astype(BF16)) + b_ref[...]


def _modulation(c, ada_w, ada_b):
    bsz, d = c.shape
    n = ada_w.shape[1]
    tn = d
    return pl.pallas_call(
        _mod_body,
        grid=(n // tn,),
        in_specs=[
            pl.BlockSpec((bsz, d), lambda j: (0, 0)),
            pl.BlockSpec((d, tn), lambda j: (0, j)),
            pl.BlockSpec((1, tn), lambda j: (0, j)),
        ],
        out_specs=pl.BlockSpec((bsz, tn), lambda j: (0, j)),
        out_shape=jax.ShapeDtypeStruct((bsz, n), F32),
        compiler_params=pltpu.CompilerParams(
            dimension_semantics=("arbitrary",), vmem_limit_bytes=_vmem_limit(2 * d * tn * 4)),
        name="adaln_modulation",
    )(c, ada_w, ada_b.reshape(1, n))


def _t5_bucket_band():
    qi = np.arange(WINDOW)[:, None]
    kj = np.arange(2 * WINDOW)[None, :]
    dist = np.maximum(WINDOW + qi - kj, 0)
    max_exact = REL_BUCKETS // 2
    scaled = np.log(np.maximum(dist, 1).astype(np.float32) / np.float32(max_exact)) / np.float32(
        math.log(REL_MAX_DIST / max_exact))
    large = max_exact + (scaled.astype(np.float32) * np.float32(REL_BUCKETS - max_exact)).astype(np.int32)
    large = np.minimum(large, REL_BUCKETS - 1)
    return np.where(dist < max_exact, dist, large).astype(np.int32)


def _bias_body(rb_ref, bucket_ref, o_ref):
    h = pl.program_id(0)
    bucket = bucket_ref[...]
    acc = jnp.zeros(bucket.shape, F32)
    for v in range(REL_BUCKETS):
        acc = jnp.where(bucket == v, rb_ref[v, h], acc)
    o_ref[0] = acc


def _rel_bias_table(rel_bias):
    bucket = jnp.asarray(_t5_bucket_band())
    return pl.pallas_call(
        _bias_body,
        grid=(ATTN_Q_HEADS,),
        in_specs=[
            pl.BlockSpec(memory_space=pltpu.SMEM),
            pl.BlockSpec((WINDOW, 2 * WINDOW), lambda h: (0, 0)),
        ],
        out_specs=pl.BlockSpec((1, WINDOW, 2 * WINDOW), lambda h: (h, 0, 0)),
        out_shape=jax.ShapeDtypeStruct((ATTN_Q_HEADS, WINDOW, 2 * WINDOW), F32),
        compiler_params=pltpu.CompilerParams(dimension_semantics=("arbitrary",)),
        name="rel_bias_table",
    )(rel_bias.astype(F32), bucket)


_IN_GROUPS = (
    ("q", ATTN_Q_DIM, BF16),
    ("k", ATTN_KV_DIM, BF16),
    ("v", ATTN_KV_DIM, BF16),
    ("dqkv", 3 * DN_DIM, BF16),
    ("dz", DN_DIM, BF16),
    ("ga", None, BF16),
    ("gd", None, BF16),
    ("bd", BD_PAD, F32),
)


def _in_groups(d_model):
    return tuple((n, d_model if w is None else w, dt) for n, w, dt in _IN_GROUPS)


def _pack_w_in(w_in):
    d = w_in.shape[0]
    o = 0
    aq = w_in[:, o:o + ATTN_Q_DIM] * (ATTN_HEAD_DIM ** -0.5); o += ATTN_Q_DIM
    ak = w_in[:, o:o + ATTN_KV_DIM]; o += ATTN_KV_DIM
    av = w_in[:, o:o + ATTN_KV_DIM]; o += ATTN_KV_DIM
    dqkv = w_in[:, o:o + 3 * DN_DIM]; o += 3 * DN_DIM
    dz = w_in[:, o:o + DN_DIM]; o += DN_DIM
    bd = w_in[:, o:o + 2 * DN_HEADS]; o += 2 * DN_HEADS
    ga = w_in[:, o:o + d]; o += d
    gd = w_in[:, o:o + d]; o += d
    bd = jnp.pad(bd, ((0, 0), (0, BD_PAD - 2 * DN_HEADS)))
    return jnp.concatenate([aq, ak, av, dqkv, dz, ga, gd, bd], axis=1).astype(BF16)


def _inproj_body(groups, x_ref, sc_ref, sh_ref, nw_ref, w_ref, *out_refs):
    xf = x_ref[0]
    u = (_rms(xf, nw_ref[...]) * (1.0 + sc_ref[0]) + sh_ref[0]).astype(BF16)
    off = 0
    for (_, width, dt), o_ref in zip(groups, out_refs):
        o_ref[0] = _dot(u, w_ref[:, off:off + width]).astype(dt)
        off += width


def _in_projection(x, sc1, sh1, norm_w, w_packed):
    bsz, seq, d = x.shape
    groups = _in_groups(d)
    n_tot = sum(w for _, w, _ in groups)
    tm = min(ROW_TILE, seq)
    row = lambda width: pl.BlockSpec((1, tm, width), lambda b, t: (b, t, 0))
    per_batch = pl.BlockSpec((1, 1, d), lambda b, t: (b, 0, 0))
    out_bytes = sum(2 * tm * w * jnp.dtype(dt).itemsize for _, w, dt in groups)
    est = 2 * tm * d * 4 + d * n_tot * 2 + out_bytes + 4 * tm * d * 4
    return pl.pallas_call(
        functools.partial(_inproj_body, groups),
        grid=(bsz, seq // tm),
        in_specs=[row(d), per_batch, per_batch, _resident((1, d)), _resident((d, n_tot))],
        out_specs=[row(w) for _, w, _ in groups],
        out_shape=[jax.ShapeDtypeStruct((bsz, seq, w), dt) for _, w, dt in groups],
        compiler_params=pltpu.CompilerParams(
            dimension_semantics=("arbitrary", "arbitrary"), vmem_limit_bytes=_vmem_limit(est)),
        name="in_projection",
    )(x, sc1, sh1, norm_w.reshape(1, d), w_packed)


def _attn_body(sink_ref, q_ref, kp_ref, kc_ref, vp_ref, vc_ref, bias_ref, o_ref):
    n = pl.program_id(1)
    q = q_ref[0]
    kb = jnp.concatenate([kp_ref[0], kc_ref[0]], axis=0)
    vb = jnp.concatenate([vp_ref[0], vc_ref[0]], axis=0)
    qi = lax.broadcasted_iota(jnp.int32, (WINDOW, 2 * WINDOW), 0)
    kj = lax.broadcasted_iota(jnp.int32, (WINDOW, 2 * WINDOW), 1)
    dist = WINDOW + qi - kj
    valid = (dist >= 0) & (dist < WINDOW) & ((kj >= WINDOW) | (n > 0))
    outs = []
    for h in range(ATTN_Q_HEADS):
        kv = h // ATTN_GROUP
        qh = q[:, h * ATTN_HEAD_DIM:(h + 1) * ATTN_HEAD_DIM]
        kh = kb[:, kv * ATTN_HEAD_DIM:(kv + 1) * ATTN_HEAD_DIM]
        vh = vb[:, kv * ATTN_HEAD_DIM:(kv + 1) * ATTN_HEAD_DIM]
        s = jnp.where(valid, _dot_nt(qh, kh) + bias_ref[h], NEG_INF)
        sink = sink_ref[h]
        m = jnp.maximum(jnp.max(s, axis=-1, keepdims=True), sink)
        p = jnp.exp(s - m)
        denom = jnp.sum(p, axis=-1, keepdims=True) + jnp.exp(sink - m)
        outs.append(_dot(p.astype(BF16), vh) / denom)
    o_ref[0] = jnp.concatenate(outs, axis=1).astype(o_ref.dtype)


def _attention(q, k, v, sinks, bias):
    bsz, seq, _ = q.shape
    nb = seq // WINDOW
    cur = lambda width: pl.BlockSpec((1, WINDOW, width), lambda b, n: (b, n, 0))
    prev = lambda width: pl.BlockSpec((1, WINDOW, width), lambda b, n: (b, jnp.maximum(n - 1, 0), 0))
    return pl.pallas_call(
        _attn_body,
        grid=(bsz, nb),
        in_specs=[
            pl.BlockSpec(memory_space=pltpu.SMEM),
            cur(ATTN_Q_DIM), prev(ATTN_KV_DIM), cur(ATTN_KV_DIM), prev(ATTN_KV_DIM), cur(ATTN_KV_DIM),
            _resident((ATTN_Q_HEADS, WINDOW, 2 * WINDOW)),
        ],
        out_specs=cur(ATTN_Q_DIM),
        out_shape=jax.ShapeDtypeStruct((bsz, seq, ATTN_Q_DIM), BF16),
        compiler_params=pltpu.CompilerParams(dimension_semantics=("arbitrary", "arbitrary")),
        name="swa_attention",
    )(sinks.astype(F32), q, k, k, v, v, bias)


def _split3(a):
    hi = a.astype(BF16)
    r = a - hi.astype(F32)
    mid = r.astype(BF16)
    lo = (r - mid.astype(F32)).astype(BF16)
    return hi, mid, lo


def _mm(a, b, dot=_dot):
    return dot(a.astype(BF16), b.astype(BF16))


def _unit_lower_inverse(lower, ii, jj):
    size = lower.shape[0]
    inv = jnp.where(ii == jj, 1.0, 0.0)
    s = 1
    while s < size:
        sh = int(math.log2(s))
        joins = (jnp.right_shift(ii, sh + 1) == jnp.right_shift(jj, sh + 1)) & (
            jnp.right_shift(ii, sh) != jnp.right_shift(jj, sh))
        e = jnp.where(joins, lower, 0.0)
        if s == 1:
            inv = inv - e
        else:
            inv = inv - _mm(inv, _mm(e, inv))
        s *= 2
    return inv


def _dn_body(qkv_ref, z_ref, bd_ref, cw_ref, alog_ref, dtb_ref, nw_ref, o_ref, state_ref, xpad_ref):
    rows = qkv_ref.shape[1]
    n_chunks = rows // DN_CHUNK
    cc = DN_CHUNK
    hd = DN_HEAD_DIM

    @pl.when(pl.program_id(1) == 0)
    def _():
        state_ref[...] = jnp.zeros_like(state_ref)
        xpad_ref[0:V7X_SUBLANES, :] = jnp.zeros((V7X_SUBLANES, xpad_ref.shape[1]), F32)

    x = qkv_ref[0].astype(F32)
    xpad_ref[V7X_SUBLANES:V7X_SUBLANES + rows, :] = x
    acc = cw_ref[DN_CONV - 1:DN_CONV, :] * x
    for j in range(DN_CONV - 1):
        start = V7X_SUBLANES - (DN_CONV - 1) + j
        acc = acc + cw_ref[j:j + 1, :] * xpad_ref[start:start + rows, :]
    xpad_ref[0:V7X_SUBLANES, :] = x[rows - V7X_SUBLANES:rows, :]
    a = _silu(acc)

    bd = bd_ref[0]
    beta_all = _sigmoid(bd)
    g_all = -jnp.exp(alog_ref[...]) * jax.nn.softplus(bd + dtb_ref[...])
    ri = lax.broadcasted_iota(jnp.int32, (rows, rows), 0)
    ci = lax.broadcasted_iota(jnp.int32, (rows, rows), 1)
    same_chunk = jnp.right_shift(ri, int(math.log2(cc))) == jnp.right_shift(ci, int(math.log2(cc)))
    tri = jnp.where(same_chunk & (ci <= ri), 1.0, 0.0).astype(BF16)
    g_hi, g_mid, g_lo = _split3(g_all)
    gc = _dot(tri, g_hi) + _dot(tri, g_mid) + _dot(tri, g_lo)
    gc_t = gc.T
    eg_all = jnp.exp(gc)

    ii = lax.broadcasted_iota(jnp.int32, (cc, cc), 0)
    jj = lax.broadcasted_iota(jnp.int32, (cc, cc), 1)
    incl = jj <= ii
    strict = jj < ii
    nw = nw_ref[...]

    for h in range(DN_HEADS):
        qh_all = a[:, h * hd:(h + 1) * hd]
        kh_all = a[:, DN_DIM + h * hd:DN_DIM + (h + 1) * hd]
        vh_all = a[:, 2 * DN_DIM + h * hd:2 * DN_DIM + (h + 1) * hd]
        qh_all = qh_all * lax.rsqrt(jnp.sum(qh_all * qh_all, axis=-1, keepdims=True) + L2_EPS) * (hd ** -0.5)
        kh_all = kh_all * lax.rsqrt(jnp.sum(kh_all * kh_all, axis=-1, keepdims=True) + L2_EPS)
        zh_all = z_ref[0, :, h * hd:(h + 1) * hd].astype(F32)
        state = state_ref[h]
        for c in range(n_chunks):
            r0 = c * cc
            q = qh_all[r0:r0 + cc]
            k = kh_all[r0:r0 + cc]
            v = vh_all[r0:r0 + cc]
            beta = beta_all[r0:r0 + cc, h:h + 1]
            gcc = gc[r0:r0 + cc, DN_HEADS + h:DN_HEADS + h + 1]
            gcr = gc_t[DN_HEADS + h:DN_HEADS + h + 1, r0:r0 + cc]
            eg = eg_all[r0:r0 + cc, DN_HEADS + h:DN_HEADS + h + 1]
            g_end = gc[r0 + cc - 1:r0 + cc, DN_HEADS + h:DN_HEADS + h + 1]
            decay = jnp.where(incl, jnp.exp(jnp.where(incl, gcc - gcr, 0.0)), 0.0)
            kb = k * beta
            vb = v * beta
            lower = jnp.where(strict, _mm(kb, k, _dot_nt) * decay, 0.0)
            intra = jnp.where(incl, _mm(q, k, _dot_nt) * decay, 0.0)
            t_inv = _unit_lower_inverse(lower, ii, jj)
            sol = _mm(t_inv, jnp.concatenate([vb, kb * eg], axis=1))
            u = sol[:, :hd]
            w = sol[:, hd:]
            v_new = u - _mm(w, state)
            o = _mm(q * eg, state) + _mm(intra, v_new)
            k_tail = k * jnp.exp(g_end - gcc)
            state = state * jnp.exp(g_end) + _mm(k_tail, v_new, _dot_tn)
            y = _rms(o, nw) * _silu(zh_all[r0:r0 + cc])
            o_ref[0, r0:r0 + cc, h * hd:(h + 1) * hd] = y.astype(o_ref.dtype)
        state_ref[h] = state


def _deltanet(dqkv, dz, bd, conv_w, a_log, dt_bias, norm_w):
    bsz, seq, _ = dqkv.shape
    rows = min(DN_ROWS, seq)
    lane_row = lambda vec: jnp.pad(vec.astype(F32), (DN_HEADS, BD_PAD - 2 * DN_HEADS)).reshape(1, BD_PAD)
    blk = lambda width: pl.BlockSpec((1, rows, width), lambda b, s: (b, s, 0))
    est = 2 * rows * (3 * DN_DIM * 2 + DN_DIM * 2 + BD_PAD * 4 + DN_DIM * 2) + 12 * rows * 3 * DN_DIM * 4
    return pl.pallas_call(
        _dn_body,
        grid=(bsz, seq // rows),
        in_specs=[
            blk(3 * DN_DIM), blk(DN_DIM), blk(BD_PAD),
            _resident((DN_CONV, 3 * DN_DIM)), _resident((1, BD_PAD)), _resident((1, BD_PAD)),
            _resident((1, DN_HEAD_DIM)),
        ],
        out_specs=blk(DN_DIM),
        out_shape=jax.ShapeDtypeStruct((bsz, seq, DN_DIM), BF16),
        scratch_shapes=[
            pltpu.VMEM((DN_HEADS, DN_HEAD_DIM, DN_HEAD_DIM), F32),
            pltpu.VMEM((rows + V7X_SUBLANES, 3 * DN_DIM), F32),
        ],
        compiler_params=pltpu.CompilerParams(
            dimension_semantics=("arbitrary", "arbitrary"), vmem_limit_bytes=_vmem_limit(est)),
        name="gated_deltanet",
    )(dqkv, dz, bd, conv_w.astype(F32), lane_row(a_log), lane_row(dt_bias), norm_w.astype(F32).reshape(1, -1))


def _mix_body(x_ref, ya_ref, yd_ref, ga_ref, gd_ref, wab_ref, wdb_ref, wo_ref, npost_ref, g1_ref, npre_ref,
              sc2_ref, sh2_ref, h_ref, u_ref):
    ba = _dot(ya_ref[0], wab_ref[...])
    bdn = _dot(yd_ref[0], wdb_ref[...])
    merged = _sigmoid(ga_ref[0].astype(F32)) * ba + _sigmoid(gd_ref[0].astype(F32)) * bdn
    y = _dot(merged.astype(BF16), wo_ref[...])
    h1 = x_ref[0] + g1_ref[0] * _rms(y, npost_ref[...])
    h_ref[0] = h1
    u_ref[0] = (_rms(h1, npre_ref[...]) * (1.0 + sc2_ref[0]) + sh2_ref[0]).astype(u_ref.dtype)


def _mix_out(x, ya, yd, ga, gd, w_ab, w_db, w_out, norm_post, g1, norm_pre, sc2, sh2):
    bsz, seq, d = x.shape
    tm = min(ROW_TILE, seq)
    row = lambda width: pl.BlockSpec((1, tm, width), lambda b, t: (b, t, 0))
    per_batch = pl.BlockSpec((1, 1, d), lambda b, t: (b, 0, 0))
    est = (2 * tm * d * (4 + 2 + 2 + 4 + 2) + 2 * tm * 2 * ATTN_Q_DIM * 2 + (2 * ATTN_Q_DIM + d) * d * 2
           + 6 * tm * d * 4)
    return pl.pallas_call(
        _mix_body,
        grid=(bsz, seq // tm),
        in_specs=[
            row(d), row(ATTN_Q_DIM), row(DN_DIM), row(d), row(d),
            _resident((ATTN_Q_DIM, d)), _resident((DN_DIM, d)), _resident((d, d)),
            _resident((1, d)), per_batch, _resident((1, d)), per_batch, per_batch,
        ],
        out_specs=[row(d), row(d)],
        out_shape=[jax.ShapeDtypeStruct((bsz, seq, d), F32), jax.ShapeDtypeStruct((bsz, seq, d), BF16)],
        compiler_params=pltpu.CompilerParams(
            dimension_semantics=("arbitrary", "arbitrary"), vmem_limit_bytes=_vmem_limit(est)),
        name="merge_out_projection",
    )(x, ya, yd, ga, gd, w_ab, w_db, w_out, norm_post.reshape(1, d), g1, norm_pre.reshape(1, d), sc2, sh2)


def _ffn_body(u_ref, h_ref, wup_ref, cw_ref, wdn_ref, npost_ref, g2_ref, o_ref, gpad_ref, vpad_ref, gcar_ref,
              vcar_ref, act_ref):
    tm = u_ref.shape[1]
    n_ch = D_FF // FF_CHUNK
    sub = V7X_SUBLANES

    @pl.when(pl.program_id(1) == 0)
    def _():
        gcar_ref[...] = jnp.zeros_like(gcar_ref)
        vcar_ref[...] = jnp.zeros_like(vcar_ref)

    u = u_ref[0]

    def conv_half(pad_ref, car_ref, j, off):
        up = _dot(u, wup_ref[:, off:off + FF_CHUNK])
        pad_ref[0:sub, :] = car_ref[j]
        pad_ref[sub:sub + tm, :] = up
        car_ref[j] = up[tm - sub:tm, :]
        out = cw_ref[FFN_CONV - 1:FFN_CONV, off:off + FF_CHUNK] * up
        for t in range(FFN_CONV - 1):
            start = sub - (FFN_CONV - 1) + t
            out = out + cw_ref[t:t + 1, off:off + FF_CHUNK] * pad_ref[start:start + tm, :]
        return out

    for j in range(n_ch):
        gate = conv_half(gpad_ref, gcar_ref, j, j * FF_CHUNK)
        val = conv_half(vpad_ref, vcar_ref, j, D_FF + j * FF_CHUNK)
        act_ref[:, j * FF_CHUNK:(j + 1) * FF_CHUNK] = (jax.nn.gelu(gate, approximate=True) * val).astype(BF16)

    y = _dot(act_ref[...], wdn_ref[...])
    o_ref[0] = h_ref[0] + g2_ref[0] * _rms(y, npost_ref[...])


def _ffn(u2, h1, w_up, conv_w, w_down, norm_post, g2):
    bsz, seq, d = h1.shape
    tm = min(ROW_TILE, seq)
    n_ch = D_FF // FF_CHUNK
    row = lambda width: pl.BlockSpec((1, tm, width), lambda b, t: (b, t, 0))
    per_batch = pl.BlockSpec((1, 1, d), lambda b, t: (b, 0, 0))
    est = (2 * tm * d * (2 + 4 + 4) + 3 * d * D_FF * 2 + 2 * (tm + 8) * FF_CHUNK * 4 + tm * D_FF * 2
           + 8 * tm * FF_CHUNK * 4 + 2 * tm * d * 4)
    return pl.pallas_call(
        _ffn_body,
        grid=(bsz, seq // tm),
        in_specs=[
            row(d), row(d),
            _resident((d, 2 * D_FF)), _resident((FFN_CONV, 2 * D_FF)), _resident((D_FF, d)),
            _resident((1, d)), per_batch,
        ],
        out_specs=row(d),
        out_shape=jax.ShapeDtypeStruct((bsz, seq, d), F32),
        scratch_shapes=[
            pltpu.VMEM((tm + V7X_SUBLANES, FF_CHUNK), F32),
            pltpu.VMEM((tm + V7X_SUBLANES, FF_CHUNK), F32),
            pltpu.VMEM((n_ch, V7X_SUBLANES, FF_CHUNK), F32),
            pltpu.VMEM((n_ch, V7X_SUBLANES, FF_CHUNK), F32),
            pltpu.VMEM((tm, D_FF), BF16),
        ],
        compiler_params=pltpu.CompilerParams(
            dimension_semantics=("arbitrary", "arbitrary"), vmem_limit_bytes=_vmem_limit(est)),
        name="conv_geglu_ffn",
    )(u2, h1, w_up, conv_w.astype(F32), w_down, norm_post.reshape(1, d), g2)


def _layer(h, c, ada_w, ada_b, norm_mix_pre, norm_mix_post, norm_ffn_pre, norm_ffn_post, w_in, dn_conv_w,
           dn_a_log, dn_dt_bias, dn_norm_w, attn_sinks, bias, w_attn_branch, w_dn_branch, w_out, ffn_w_up,
           ffn_conv_w, ffn_w_down):
    bsz, seq, d = h.shape
    mod = _modulation(c, ada_w, ada_b)
    sh1, sc1, g1, sh2, sc2, g2 = [mod[:, i * d:(i + 1) * d].reshape(bsz, 1, d) for i in range(N_MOD)]
    q, k, v, dqkv, dz, ga, gd, bd = _in_projection(h, sc1, sh1, norm_mix_pre, _pack_w_in(w_in))
    y_attn = _attention(q, k, v, attn_sinks, bias)
    y_dn = _deltanet(dqkv, dz, bd, dn_conv_w, dn_a_log, dn_dt_bias, dn_norm_w)
    h1, u2 = _mix_out(h, y_attn, y_dn, ga, gd, w_attn_branch.astype(BF16), w_dn_branch.astype(BF16),
                      w_out.astype(BF16), norm_mix_post, g1, norm_ffn_pre, sc2, sh2)
    return _ffn(u2, h1, ffn_w_up.astype(BF16), ffn_conv_w, ffn_w_down.astype(BF16), norm_ffn_post, g2)


def kernel(x, c, ada_w, ada_b, norm_mix_pre, norm_mix_post, norm_ffn_pre, norm_ffn_post, w_in, dn_conv_w, dn_a_log, dn_dt_bias, dn_norm_w, attn_sinks, rel_bias, w_attn_branch, w_dn_branch, w_out, ffn_w_up, ffn_conv_w, ffn_w_down):
    bias = _rel_bias_table(rel_bias)
    h = x
    for l in range(ada_w.shape[0]):
        h = _layer(h, c, ada_w[l], ada_b[l], norm_mix_pre[l], norm_mix_post[l], norm_ffn_pre[l],
                   norm_ffn_post[l], w_in[l], dn_conv_w[l], dn_a_log[l], dn_dt_bias[l], dn_norm_w[l],
                   attn_sinks[l], bias, w_attn_branch[l], w_dn_branch[l], w_out[l], ffn_w_up[l],
                   ffn_conv_w[l], ffn_w_down[l])
    return h
```

```python
import functools
import math

import numpy as np
import jax
import jax.numpy as jnp
from jax import lax
from jax.experimental import pallas as pl
from jax.experimental.pallas import tpu as pltpu

ATTN_Q_HEADS = 8
ATTN_KV_HEADS = 2
ATTN_HEAD_DIM = 64
WINDOW = 128
REL_BUCKETS = 32
REL_MAX_DIST = 128
DN_HEADS = 4
DN_HEAD_DIM = 128
DN_CONV = 4
DN_CHUNK = 64
D_FF = 2816
FFN_CONV = 3
RMS_EPS = 1e-6
L2_EPS = 1e-6
N_MOD = 6
NEG_INF = -1e30

ATTN_Q_DIM = ATTN_Q_HEADS * ATTN_HEAD_DIM
ATTN_KV_DIM = ATTN_KV_HEADS * ATTN_HEAD_DIM
DN_DIM = DN_HEADS * DN_HEAD_DIM
ATTN_GROUP = ATTN_Q_HEADS // ATTN_KV_HEADS

V7X_LANES = 128
V7X_SUBLANES = 8
V7X_VMEM_BYTES = 64 * 1024 * 1024

F32 = jnp.float32
BF16 = jnp.bfloat16

ROW_TILE = 512
DN_ROWS = 256
FF_CHUNK = 256
BD_PAD = V7X_LANES


def _vmem_limit(nbytes):
    return int(min(V7X_VMEM_BYTES, nbytes + nbytes // 2 + (8 << 20)))


def _resident(shape):
    nd = len(shape)
    return pl.BlockSpec(shape, lambda *_: (0,) * nd, pipeline_mode=pl.Buffered(1))


def _dot(a, b):
    return jnp.dot(a, b, preferred_element_type=F32)


def _dot_nt(a, b):
    return lax.dot_general(a, b, (((1,), (1,)), ((), ())), preferred_element_type=F32)


def _dot_tn(a, b):
    return lax.dot_general(a, b, (((0,), (0,)), ((), ())), preferred_element_type=F32)


def _rms(xf, w):
    return xf * lax.rsqrt(jnp.mean(xf * xf, axis=-1, keepdims=True) + RMS_EPS) * w


def _sigmoid(x):
    return 1.0 / (1.0 + jnp.exp(-x))


def _silu(x):
    return x * _sigmoid(x)


def _mod_body(c_ref, w_ref, b_ref, o_ref):
    ca = _silu(c_ref[...])
    o_ref[...] = _dot(ca.astype(BF16), w_ref[...].astype(BF16)) + b_ref[...]


def _modulation(c, ada_w, ada_b):
    bsz, d = c.shape
    n = ada_w.shape[1]
    tn = d
    return pl.pallas_call(
        _mod_body,
        grid=(n // tn,),
        in_specs=[
            pl.BlockSpec((bsz, d), lambda j: (0, 0)),
            pl.BlockSpec((d, tn), lambda j: (0, j)),
            pl.BlockSpec((1, tn), lambda j: (0, j)),
        ],
        out_specs=pl.BlockSpec((bsz, tn), lambda j: (0, j)),
        out_shape=jax.ShapeDtypeStruct((bsz, n), F32),
        compiler_params=pltpu.CompilerParams(
            dimension_semantics=("arbitrary",), vmem_limit_bytes=_vmem_limit(2 * d * tn * 4)),
        name="adaln_modulation",
    )(c, ada_w, ada_b.reshape(1, n))


def _t5_bucket_band():
    qi = np.arange(WINDOW)[:, None]
    kj = np.arange(2 * WINDOW)[None, :]
    dist = np.maximum(WINDOW + qi - kj, 0)
    max_exact = REL_BUCKETS // 2
    scaled = np.log(np.maximum(dist, 1).astype(np.float32) / np.float32(max_exact)) / np.float32(
        math.log(REL_MAX_DIST / max_exact))
    large = max_exact + (scaled.astype(np.float32) * np.float32(REL_BUCKETS - max_exact)).astype(np.int32)
    large = np.minimum(large, REL_BUCKETS - 1)
    return np.where(dist < max_exact, dist, large).astype(np.int32)


def _bias_body(rb_ref, bucket_ref, o_ref):
    h = pl.program_id(0)
    bucket = bucket_ref[...]
    acc = jnp.zeros(bucket.shape, F32)
    for v in range(REL_BUCKETS):
        acc = jnp.where(bucket == v, rb_ref[v, h], acc)
    o_ref[0] = acc


def _rel_bias_table(rel_bias):
    bucket = jnp.asarray(_t5_bucket_band())
    return pl.pallas_call(
        _bias_body,
        grid=(ATTN_Q_HEADS,),
        in_specs=[
            pl.BlockSpec(memory_space=pltpu.SMEM),
            pl.BlockSpec((WINDOW, 2 * WINDOW), lambda h: (0, 0)),
        ],
        out_specs=pl.BlockSpec((1, WINDOW, 2 * WINDOW), lambda h: (h, 0, 0)),
        out_shape=jax.ShapeDtypeStruct((ATTN_Q_HEADS, WINDOW, 2 * WINDOW), F32),
        compiler_params=pltpu.CompilerParams(dimension_semantics=("arbitrary",)),
        name="rel_bias_table",
    )(rel_bias.astype(F32), bucket)


_IN_GROUPS = (
    ("q", ATTN_Q_DIM, BF16),
    ("k", ATTN_KV_DIM, BF16),
    ("v", ATTN_KV_DIM, BF16),
    ("dqkv", 3 * DN_DIM, BF16),
    ("dz", DN_DIM, BF16),
    ("ga", None, BF16),
    ("gd", None, BF16),
    ("bd", BD_PAD, F32),
)


def _in_groups(d_model):
    return tuple((n, d_model if w is None else w, dt) for n, w, dt in _IN_GROUPS)


def _pack_w_in(w_in):
    d = w_in.shape[0]
    o = 0
    aq = w_in[:, o:o + ATTN_Q_DIM] * (ATTN_HEAD_DIM ** -0.5); o += ATTN_Q_DIM
    ak = w_in[:, o:o + ATTN_KV_DIM]; o += ATTN_KV_DIM
    av = w_in[:, o:o + ATTN_KV_DIM]; o += ATTN_KV_DIM
    dqkv = w_in[:, o:o + 3 * DN_DIM]; o += 3 * DN_DIM
    dz = w_in[:, o:o + DN_DIM]; o += DN_DIM
    bd = w_in[:, o:o + 2 * DN_HEADS]; o += 2 * DN_HEADS
    ga = w_in[:, o:o + d]; o += d
    gd = w_in[:, o:o + d]; o += d
    bd = jnp.pad(bd, ((0, 0), (0, BD_PAD - 2 * DN_HEADS)))
    return jnp.concatenate([aq, ak, av, dqkv, dz, ga, gd, bd], axis=1).astype(BF16)


def _inproj_body(groups, x_ref, sc_ref, sh_ref, nw_ref, w_ref, *out_refs):
    xf = x_ref[0]
    u = (_rms(xf, nw_ref[...]) * (1.0 + sc_ref[0]) + sh_ref[0]).astype(BF16)
    off = 0
    for (_, width, dt), o_ref in zip(groups, out_refs):
        o_ref[0] = _dot(u, w_ref[:, off:off + width]).astype(dt)
        off += width


def _in_projection(x, sc1, sh1, norm_w, w_packed):
    bsz, seq, d = x.shape
    groups = _in_groups(d)
    n_tot = sum(w for _, w, _ in groups)
    tm = min(ROW_TILE, seq)
    row = lambda width: pl.BlockSpec((1, tm, width), lambda b, t: (b, t, 0))
    per_batch = pl.BlockSpec((1, 1, d), lambda b, t: (b, 0, 0))
    out_bytes = sum(2 * tm * w * jnp.dtype(dt).itemsize for _, w, dt in groups)
    est = 2 * tm * d * 4 + d * n_tot * 2 + out_bytes + 4 * tm * d * 4
    return pl.pallas_call(
        functools.partial(_inproj_body, groups),
        grid=(bsz, seq // tm),
        in_specs=[row(d), per_batch, per_batch, _resident((1, d)), _resident((d, n_tot))],
        out_specs=[row(w) for _, w, _ in groups],
        out_shape=[jax.ShapeDtypeStruct((bsz, seq, w), dt) for _, w, dt in groups],
        compiler_params=pltpu.CompilerParams(
            dimension_semantics=("arbitrary", "arbitrary"), vmem_limit_bytes=_vmem_limit(est)),
        name="in_projection",
    )(x, sc1, sh1, norm_w.reshape(1, d), w_packed)


def _attn_body(sink_ref, q_ref, kp_ref, kc_ref, vp_ref, vc_ref, bias_ref, o_ref):
    n = pl.program_id(1)
    q = q_ref[0]
    kb = jnp.concatenate([kp_ref[0], kc_ref[0]], axis=0)
    vb = jnp.concatenate([vp_ref[0], vc_ref[0]], axis=0)
    qi = lax.broadcasted_iota(jnp.int32, (WINDOW, 2 * WINDOW), 0)
    kj = lax.broadcasted_iota(jnp.int32, (WINDOW, 2 * WINDOW), 1)
    dist = WINDOW + qi - kj
    valid = (dist >= 0) & (dist < WINDOW) & ((kj >= WINDOW) | (n > 0))
    outs = []
    for h in range(ATTN_Q_HEADS):
        kv = h // ATTN_GROUP
        qh = q[:, h * ATTN_HEAD_DIM:(h + 1) * ATTN_HEAD_DIM]
        kh = kb[:, kv * ATTN_HEAD_DIM:(kv + 1) * ATTN_HEAD_DIM]
        vh = vb[:, kv * ATTN_HEAD_DIM:(kv + 1) * ATTN_HEAD_DIM]
        s = jnp.where(valid, _dot_nt(qh, kh) + bias_ref[h], NEG_INF)
        sink = sink_ref[h]
        m = jnp.maximum(jnp.max(s, axis=-1, keepdims=True), sink)
        p = jnp.exp(s - m)
        denom = jnp.sum(p, axis=-1, keepdims=True) + jnp.exp(sink - m)
        outs.append(_dot(p.astype(BF16), vh) / denom)
    o_ref[0] = jnp.concatenate(outs, axis=1).astype(o_ref.dtype)


def _attention(q, k, v, sinks, bias):
    bsz, seq, _ = q.shape
    nb = seq // WINDOW
    cur = lambda width: pl.BlockSpec((1, WINDOW, width), lambda b, n: (b, n, 0))
    prev = lambda width: pl.BlockSpec((1, WINDOW, width), lambda b, n: (b, jnp.maximum(n - 1, 0), 0))
    return pl.pallas_call(
        _attn_body,
        grid=(bsz, nb),
        in_specs=[
            pl.BlockSpec(memory_space=pltpu.SMEM),
            cur(ATTN_Q_DIM), prev(ATTN_KV_DIM), cur(ATTN_KV_DIM), prev(ATTN_KV_DIM), cur(ATTN_KV_DIM),
            _resident((ATTN_Q_HEADS, WINDOW, 2 * WINDOW)),
        ],
        out_specs=cur(ATTN_Q_DIM),
        out_shape=jax.ShapeDtypeStruct((bsz, seq, ATTN_Q_DIM), BF16),
        compiler_params=pltpu.CompilerParams(dimension_semantics=("arbitrary", "arbitrary")),
        name="swa_attention",
    )(sinks.astype(F32), q, k, k, v, v, bias)


def _split3(a):
    hi = a.astype(BF16)
    r = a - hi.astype(F32)
    mid = r.astype(BF16)
    lo = (r - mid.astype(F32)).astype(BF16)
    return hi, mid, lo


def _mm(a, b, dot=_dot):
    return dot(a.astype(BF16), b.astype(BF16))


def _unit_lower_inverses(lowers, ii, jj):
    size = lowers[0].shape[0]
    eye = jnp.where(ii == jj, 1.0, 0.0)
    invs = [eye] * len(lowers)
    s = 1
    while s < size:
        sh = int(math.log2(s))
        joins = (jnp.right_shift(ii, sh + 1) == jnp.right_shift(jj, sh + 1)) & (
            jnp.right_shift(ii, sh) != jnp.right_shift(jj, sh))
        es = [jnp.where(joins, lower, 0.0) for lower in lowers]
        if s == 1:
            invs = [inv - e for inv, e in zip(invs, es)]
        else:
            tmps = [_mm(e, inv) for e, inv in zip(es, invs)]
            invs = [inv - _mm(inv, tmp) for inv, tmp in zip(invs, tmps)]
        s *= 2
    return invs


def _dn_body(qkv_ref, z_ref, bd_ref, cw_ref, alog_ref, dtb_ref, nw_ref, o_ref, state_ref, xpad_ref):
    rows = qkv_ref.shape[1]
    hd = DN_HEAD_DIM

    @pl.when(pl.program_id(1) == 0)
    def _():
        state_ref[...] = jnp.zeros_like(state_ref)
        xpad_ref[0:V7X_SUBLANES, :] = jnp.zeros((V7X_SUBLANES, xpad_ref.shape[1]), F32)

    x = qkv_ref[0].astype(F32)
    xpad_ref[V7X_SUBLANES:V7X_SUBLANES + rows, :] = x
    acc = cw_ref[DN_CONV - 1:DN_CONV, :] * x
    for j in range(DN_CONV - 1):
        start = V7X_SUBLANES - (DN_CONV - 1) + j
        acc = acc + cw_ref[j:j + 1, :] * xpad_ref[start:start + rows, :]
    xpad_ref[0:V7X_SUBLANES, :] = x[rows - V7X_SUBLANES:rows, :]
    a = _silu(acc)

    bd = bd_ref[0]
    beta_all = _sigmoid(bd)
    g_all = -jnp.exp(alog_ref[...]) * jax.nn.softplus(bd + dtb_ref[...])
    ii = lax.broadcasted_iota(jnp.int32, (rows, rows), 0)
    jj = lax.broadcasted_iota(jnp.int32, (rows, rows), 1)
    incl = jj <= ii
    strict = jj < ii
    tri = jnp.where(incl, 1.0, 0.0).astype(BF16)
    g_hi, g_mid, g_lo = _split3(g_all)
    gc = _dot(tri, g_hi) + _dot(tri, g_mid) + _dot(tri, g_lo)
    gc_t = gc.T
    eg_all = jnp.exp(gc)
    nw = nw_ref[...]

    heads = range(DN_HEADS)
    q_l, k_l, kb_l, rhs_l, gcc_l, eg_l, gend_l, lower_l, intra_l = [], [], [], [], [], [], [], [], []
    for h in heads:
        q = a[:, h * hd:(h + 1) * hd]
        k = a[:, DN_DIM + h * hd:DN_DIM + (h + 1) * hd]
        v = a[:, 2 * DN_DIM + h * hd:2 * DN_DIM + (h + 1) * hd]
        q = q * lax.rsqrt(jnp.sum(q * q, axis=-1, keepdims=True) + L2_EPS) * (hd ** -0.5)
        k = k * lax.rsqrt(jnp.sum(k * k, axis=-1, keepdims=True) + L2_EPS)
        beta = beta_all[:, h:h + 1]
        gcc = gc[:, DN_HEADS + h:DN_HEADS + h + 1]
        gcr = gc_t[DN_HEADS + h:DN_HEADS + h + 1, :]
        eg = eg_all[:, DN_HEADS + h:DN_HEADS + h + 1]
        decay = jnp.where(incl, jnp.exp(jnp.where(incl, gcc - gcr, 0.0)), 0.0)
        kb = k * beta
        q_l.append(q); k_l.append(k); kb_l.append(kb); gcc_l.append(gcc); eg_l.append(eg)
        gend_l.append(gc[rows - 1:rows, DN_HEADS + h:DN_HEADS + h + 1])
        rhs_l.append(jnp.concatenate([v * beta, kb * eg], axis=1))
        lower_l.append(jnp.where(strict, _mm(kb, k, _dot_nt) * decay, 0.0))
        intra_l.append(jnp.where(incl, _mm(q, k, _dot_nt) * decay, 0.0))
    inv_l = _unit_lower_inverses(lower_l, ii, jj)
    sol_l = [_mm(inv, rhs) for inv, rhs in zip(inv_l, rhs_l)]
    state_l = [state_ref[h] for h in heads]
    vnew_l = [sol[:, :hd] - _mm(sol[:, hd:], state) for sol, state in zip(sol_l, state_l)]
    for h in heads:
        o = _mm(q_l[h] * eg_l[h], state_l[h]) + _mm(intra_l[h], vnew_l[h])
        k_tail = k_l[h] * jnp.exp(gend_l[h] - gcc_l[h])
        state_ref[h] = state_l[h] * jnp.exp(gend_l[h]) + _mm(k_tail, vnew_l[h], _dot_tn)
        y = _rms(o, nw) * _silu(z_ref[0, :, h * hd:(h + 1) * hd].astype(F32))
        o_ref[0, :, h * hd:(h + 1) * hd] = y.astype(o_ref.dtype)


def _deltanet(dqkv, dz, bd, conv_w, a_log, dt_bias, norm_w):
    bsz, seq, _ = dqkv.shape
    rows = min(DN_ROWS, seq)
    lane_row = lambda vec: jnp.pad(vec.astype(F32), (DN_HEADS, BD_PAD - 2 * DN_HEADS)).reshape(1, BD_PAD)
    blk = lambda width: pl.BlockSpec((1, rows, width), lambda b, s: (b, s, 0))
    est = 2 * rows * (3 * DN_DIM * 2 + DN_DIM * 2 + BD_PAD * 4 + DN_DIM * 2) + 12 * rows * 3 * DN_DIM * 4
    return pl.pallas_call(
        _dn_body,
        grid=(bsz, seq // rows),
        in_specs=[
            blk(3 * DN_DIM), blk(DN_DIM), blk(BD_PAD),
            _resident((DN_CONV, 3 * DN_DIM)), _resident((1, BD_PAD)), _resident((1, BD_PAD)),
            _resident((1, DN_HEAD_DIM)),
        ],
        out_specs=blk(DN_DIM),
        out_shape=jax.ShapeDtypeStruct((bsz, seq, DN_DIM), BF16),
        scratch_shapes=[
            pltpu.VMEM((DN_HEADS, DN_HEAD_DIM, DN_HEAD_DIM), F32),
            pltpu.VMEM((rows + V7X_SUBLANES, 3 * DN_DIM), F32),
        ],
        compiler_params=pltpu.CompilerParams(
            dimension_semantics=("arbitrary", "arbitrary"), vmem_limit_bytes=_vmem_limit(est)),
        name="gated_deltanet",
    )(dqkv, dz, bd, conv_w.astype(F32), lane_row(a_log), lane_row(dt_bias), norm_w.astype(F32).reshape(1, -1))


def _mix_body(x_ref, ya_ref, yd_ref, ga_ref, gd_ref, wab_ref, wdb_ref, wo_ref, npost_ref, g1_ref, npre_ref,
              sc2_ref, sh2_ref, h_ref, u_ref):
    ba = _dot(ya_ref[0], wab_ref[...])
    bdn = _dot(yd_ref[0], wdb_ref[...])
    merged = _sigmoid(ga_ref[0].astype(F32)) * ba + _sigmoid(gd_ref[0].astype(F32)) * bdn
    y = _dot(merged.astype(BF16), wo_ref[...])
    h1 = x_ref[0] + g1_ref[0] * _rms(y, npost_ref[...])
    h_ref[0] = h1
    u_ref[0] = (_rms(h1, npre_ref[...]) * (1.0 + sc2_ref[0]) + sh2_ref[0]).astype(u_ref.dtype)


def _mix_out(x, ya, yd, ga, gd, w_ab, w_db, w_out, norm_post, g1, norm_pre, sc2, sh2):
    bsz, seq, d = x.shape
    tm = min(ROW_TILE, seq)
    row = lambda width: pl.BlockSpec((1, tm, width), lambda b, t: (b, t, 0))
    per_batch = pl.BlockSpec((1, 1, d), lambda b, t: (b, 0, 0))
    est = (2 * tm * d * (4 + 2 + 2 + 4 + 2) + 2 * tm * 2 * ATTN_Q_DIM * 2 + (2 * ATTN_Q_DIM + d) * d * 2
           + 6 * tm * d * 4)
    return pl.pallas_call(
        _mix_body,
        grid=(bsz, seq // tm),
        in_specs=[
            row(d), row(ATTN_Q_DIM), row(DN_DIM), row(d), row(d),
            _resident((ATTN_Q_DIM, d)), _resident((DN_DIM, d)), _resident((d, d)),
            _resident((1, d)), per_batch, _resident((1, d)), per_batch, per_batch,
        ],
        out_specs=[row(d), row(d)],
        out_shape=[jax.ShapeDtypeStruct((bsz, seq, d), F32), jax.ShapeDtypeStruct((bsz, seq, d), BF16)],
        compiler_params=pltpu.CompilerParams(
            dimension_semantics=("arbitrary", "arbitrary"), vmem_limit_bytes=_vmem_limit(est)),
        name="merge_out_projection",
    )(x, ya, yd, ga, gd, w_ab, w_db, w_out, norm_post.reshape(1, d), g1, norm_pre.reshape(1, d), sc2, sh2)


def _ffn_body(u_ref, h_ref, wup_ref, cw_ref, wdn_ref, npost_ref, g2_ref, o_ref, gpad_ref, vpad_ref, gcar_ref,
              vcar_ref, act_ref):
    tm = u_ref.shape[1]
    n_ch = D_FF // FF_CHUNK
    sub = V7X_SUBLANES

    @pl.when(pl.program_id(1) == 0)
    def _():
        gcar_ref[...] = jnp.zeros_like(gcar_ref)
        vcar_ref[...] = jnp.zeros_like(vcar_ref)

    u = u_ref[0]

    def conv_half(pad_ref, car_ref, j, off):
        up = _dot(u, wup_ref[:, off:off + FF_CHUNK])
        pad_ref[0:sub, :] = car_ref[j]
        pad_ref[sub:sub + tm, :] = up
        car_ref[j] = up[tm - sub:tm, :]
        out = cw_ref[FFN_CONV - 1:FFN_CONV, off:off + FF_CHUNK] * up
        for t in range(FFN_CONV - 1):
            start = sub - (FFN_CONV - 1) + t
            out = out + cw_ref[t:t + 1, off:off + FF_CHUNK] * pad_ref[start:start + tm, :]
        return out

    for j in range(n_ch):
        gate = conv_half(gpad_ref, gcar_ref, j, j * FF_CHUNK)
        val = conv_half(vpad_ref, vcar_ref, j, D_FF + j * FF_CHUNK)
        act_ref[:, j * FF_CHUNK:(j + 1) * FF_CHUNK] = (jax.nn.gelu(gate, approximate=True) * val).astype(BF16)

    y = _dot(act_ref[...], wdn_ref[...])
    o_ref[0] = h_ref[0] + g2_ref[0] * _rms(y, npost_ref[...])


def _ffn(u2, h1, w_up, conv_w, w_down, norm_post, g2):
    bsz, seq, d = h1.shape
    tm = min(ROW_TILE, seq)
    n_ch = D_FF // FF_CHUNK
    row = lambda width: pl.BlockSpec((1, tm, width), lambda b, t: (b, t, 0))
    per_batch = pl.BlockSpec((1, 1, d), lambda b, t: (b, 0, 0))
    est = (2 * tm * d * (2 + 4 + 4) + 3 * d * D_FF * 2 + 2 * (tm + 8) * FF_CHUNK * 4 + tm * D_FF * 2
           + 8 * tm * FF_CHUNK * 4 + 2 * tm * d * 4)
    return pl.pallas_call(
        _ffn_body,
        grid=(bsz, seq // tm),
        in_specs=[
            row(d), row(d),
            _resident((d, 2 * D_FF)), _resident((FFN_CONV, 2 * D_FF)), _resident((D_FF, d)),
            _resident((1, d)), per_batch,
        ],
        out_specs=row(d),
        out_shape=jax.ShapeDtypeStruct((bsz, seq, d), F32),
        scratch_shapes=[
            pltpu.VMEM((tm + V7X_SUBLANES, FF_CHUNK), F32),
            pltpu.VMEM((tm + V7X_SUBLANES, FF_CHUNK), F32),
            pltpu.VMEM((n_ch, V7X_SUBLANES, FF_CHUNK), F32),
            pltpu.VMEM((n_ch, V7X_SUBLANES, FF_CHUNK), F32),
            pltpu.VMEM((tm, D_FF), BF16),
        ],
        compiler_params=pltpu.CompilerParams(
            dimension_semantics=("arbitrary", "arbitrary"), vmem_limit_bytes=_vmem_limit(est)),
        name="conv_geglu_ffn",
    )(u2, h1, w_up, conv_w.astype(F32), w_down, norm_post.reshape(1, d), g2)


def _layer(h, c, ada_w, ada_b, norm_mix_pre, norm_mix_post, norm_ffn_pre, norm_ffn_post, w_in, dn_conv_w,
           dn_a_log, dn_dt_bias, dn_norm_w, attn_sinks, bias, w_attn_branch, w_dn_branch, w_out, ffn_w_up,
           ffn_conv_w, ffn_w_down):
    bsz, seq, d = h.shape
    mod = _modulation(c, ada_w, ada_b)
    sh1, sc1, g1, sh2, sc2, g2 = [mod[:, i * d:(i + 1) * d].reshape(bsz, 1, d) for i in range(N_MOD)]
    q, k, v, dqkv, dz, ga, gd, bd = _in_projection(h, sc1, sh1, norm_mix_pre, _pack_w_in(w_in))
    y_attn = _attention(q, k, v, attn_sinks, bias)
    y_dn = _deltanet(dqkv, dz, bd, dn_conv_w, dn_a_log, dn_dt_bias, dn_norm_w)
    h1, u2 = _mix_out(h, y_attn, y_dn, ga, gd, w_attn_branch.astype(BF16), w_dn_branch.astype(BF16),
                      w_out.astype(BF16), norm_mix_post, g1, norm_ffn_pre, sc2, sh2)
    return _ffn(u2, h1, ffn_w_up.astype(BF16), ffn_conv_w, ffn_w_down.astype(BF16), norm_ffn_post, g2)


def kernel(x, c, ada_w, ada_b, norm_mix_pre, norm_mix_post, norm_ffn_pre, norm_ffn_post, w_in, dn_conv_w, dn_a_log, dn_dt_bias, dn_norm_w, attn_sinks, rel_bias, w_attn_branch, w_dn_branch, w_out, ffn_w_up, ffn_conv_w, ffn_w_down):
    bias = _rel_bias_table(rel_bias)
    h = x
    for l in range(ada_w.shape[0]):
        h = _layer(h, c, ada_w[l], ada_b[l], norm_mix_pre[l], norm_mix_post[l], norm_ffn_pre[l],
                   norm_ffn_post[l], w_in[l], dn_conv_w[l], dn_a_log[l], dn_dt_bias[l], dn_norm_w[l],
                   attn_sinks[l], bias, w_attn_branch[l], w_dn_branch[l], w_out[l], ffn_w_up[l],
                   ffn_conv_w[l], ffn_w_down[l])
    return h
```

```python
import functools
import math

import numpy as np
import jax
import jax.numpy as jnp
from jax import lax
from jax.experimental import pallas as pl
from jax.experimental.pallas import tpu as pltpu

ATTN_Q_HEADS = 8
ATTN_KV_HEADS = 2
ATTN_HEAD_DIM = 64
WINDOW = 128
REL_BUCKETS = 32
REL_MAX_DIST = 128
DN_HEADS = 4
DN_HEAD_DIM = 128
DN_CONV = 4
DN_CHUNK = 64
D_FF = 2816
FFN_CONV = 3
RMS_EPS = 1e-6
L2_EPS = 1e-6
N_MOD = 6
NEG_INF = -1e30

ATTN_Q_DIM = ATTN_Q_HEADS * ATTN_HEAD_DIM
ATTN_KV_DIM = ATTN_KV_HEADS * ATTN_HEAD_DIM
DN_DIM = DN_HEADS * DN_HEAD_DIM
ATTN_GROUP = ATTN_Q_HEADS // ATTN_KV_HEADS

V7X_LANES = 128
V7X_SUBLANES = 8
V7X_VMEM_BYTES = 64 * 1024 * 1024

F32 = jnp.float32
BF16 = jnp.bfloat16

ROW_TILE = 512
DN_ROWS = 256
DN_BATCH = 2
FF_CHUNK = 256
BD_PAD = V7X_LANES


def _vmem_limit(nbytes):
    return int(min(V7X_VMEM_BYTES, nbytes + nbytes // 2 + (8 << 20)))


def _resident(shape):
    nd = len(shape)
    return pl.BlockSpec(shape, lambda *_: (0,) * nd, pipeline_mode=pl.Buffered(1))


def _dot(a, b):
    return jnp.dot(a, b, preferred_element_type=F32)


def _dot_nt(a, b):
    return lax.dot_general(a, b, (((1,), (1,)), ((), ())), preferred_element_type=F32)


def _dot_tn(a, b):
    return lax.dot_general(a, b, (((0,), (0,)), ((), ())), preferred_element_type=F32)


def _rms(xf, w):
    return xf * lax.rsqrt(jnp.mean(xf * xf, axis=-1, keepdims=True) + RMS_EPS) * w


def _sigmoid(x):
    return 1.0 / (1.0 + jnp.exp(-x))


def _silu(x):
    return x * _sigmoid(x)


def _mod_body(c_ref, w_ref, b_ref, o_ref):
    ca = _silu(c_ref[...])
    o_ref[...] = _dot(ca.astype(BF16), w_ref[...].astype(BF16)) + b_ref[...]


def _modulation(c, ada_w, ada_b):
    bsz, d = c.shape
    n = ada_w.shape[1]
    tn = d
    return pl.pallas_call(
        _mod_body,
        grid=(n // tn,),
        in_specs=[
            pl.BlockSpec((bsz, d), lambda j: (0, 0)),
            pl.BlockSpec((d, tn), lambda j: (0, j)),
            pl.BlockSpec((1, tn), lambda j: (0, j)),
        ],
        out_specs=pl.BlockSpec((bsz, tn), lambda j: (0, j)),
        out_shape=jax.ShapeDtypeStruct((bsz, n), F32),
        compiler_params=pltpu.CompilerParams(
            dimension_semantics=("arbitrary",), vmem_limit_bytes=_vmem_limit(2 * d * tn * 4)),
        name="adaln_modulation",
    )(c, ada_w, ada_b.reshape(1, n))


def _t5_bucket_band():
    qi = np.arange(WINDOW)[:, None]
    kj = np.arange(2 * WINDOW)[None, :]
    dist = np.maximum(WINDOW + qi - kj, 0)
    max_exact = REL_BUCKETS // 2
    scaled = np.log(np.maximum(dist, 1).astype(np.float32) / np.float32(max_exact)) / np.float32(
        math.log(REL_MAX_DIST / max_exact))
    large = max_exact + (scaled.astype(np.float32) * np.float32(REL_BUCKETS - max_exact)).astype(np.int32)
    large = np.minimum(large, REL_BUCKETS - 1)
    return np.where(dist < max_exact, dist, large).astype(np.int32)


def _bias_body(rb_ref, bucket_ref, o_ref):
    h = pl.program_id(0)
    bucket = bucket_ref[...]
    acc = jnp.zeros(bucket.shape, F32)
    for v in range(REL_BUCKETS):
        acc = jnp.where(bucket == v, rb_ref[v, h], acc)
    o_ref[0] = acc


def _rel_bias_table(rel_bias):
    bucket = jnp.asarray(_t5_bucket_band())
    return pl.pallas_call(
        _bias_body,
        grid=(ATTN_Q_HEADS,),
        in_specs=[
            pl.BlockSpec(memory_space=pltpu.SMEM),
            pl.BlockSpec((WINDOW, 2 * WINDOW), lambda h: (0, 0)),
        ],
        out_specs=pl.BlockSpec((1, WINDOW, 2 * WINDOW), lambda h: (h, 0, 0)),
        out_shape=jax.ShapeDtypeStruct((ATTN_Q_HEADS, WINDOW, 2 * WINDOW), F32),
        compiler_params=pltpu.CompilerParams(dimension_semantics=("arbitrary",)),
        name="rel_bias_table",
    )(rel_bias.astype(F32), bucket)


_IN_GROUPS = (
    ("q", ATTN_Q_DIM, BF16),
    ("k", ATTN_KV_DIM, BF16),
    ("v", ATTN_KV_DIM, BF16),
    ("dqkv", 3 * DN_DIM, BF16),
    ("dz", DN_DIM, BF16),
    ("ga", None, BF16),
    ("gd", None, BF16),
    ("bd", BD_PAD, F32),
)


def _in_groups(d_model):
    return tuple((n, d_model if w is None else w, dt) for n, w, dt in _IN_GROUPS)


def _pack_w_in(w_in):
    d = w_in.shape[0]
    o = 0
    aq = w_in[:, o:o + ATTN_Q_DIM] * (ATTN_HEAD_DIM ** -0.5); o += ATTN_Q_DIM
    ak = w_in[:, o:o + ATTN_KV_DIM]; o += ATTN_KV_DIM
    av = w_in[:, o:o + ATTN_KV_DIM]; o += ATTN_KV_DIM
    dqkv = w_in[:, o:o + 3 * DN_DIM]; o += 3 * DN_DIM
    dz = w_in[:, o:o + DN_DIM]; o += DN_DIM
    bd = w_in[:, o:o + 2 * DN_HEADS]; o += 2 * DN_HEADS
    ga = w_in[:, o:o + d]; o += d
    gd = w_in[:, o:o + d]; o += d
    bd = jnp.pad(bd, ((0, 0), (0, BD_PAD - 2 * DN_HEADS)))
    return jnp.concatenate([aq, ak, av, dqkv, dz, ga, gd, bd], axis=1).astype(BF16)


def _inproj_body(groups, x_ref, sc_ref, sh_ref, nw_ref, w_ref, *out_refs):
    xf = x_ref[0]
    u = (_rms(xf, nw_ref[...]) * (1.0 + sc_ref[0]) + sh_ref[0]).astype(BF16)
    off = 0
    for (_, width, dt), o_ref in zip(groups, out_refs):
        o_ref[0] = _dot(u, w_ref[:, off:off + width]).astype(dt)
        off += width


def _in_projection(x, sc1, sh1, norm_w, w_packed):
    bsz, seq, d = x.shape
    groups = _in_groups(d)
    n_tot = sum(w for _, w, _ in groups)
    tm = min(ROW_TILE, seq)
    row = lambda width: pl.BlockSpec((1, tm, width), lambda b, t: (b, t, 0))
    per_batch = pl.BlockSpec((1, 1, d), lambda b, t: (b, 0, 0))
    out_bytes = sum(2 * tm * w * jnp.dtype(dt).itemsize for _, w, dt in groups)
    est = 2 * tm * d * 4 + d * n_tot * 2 + out_bytes + 4 * tm * d * 4
    return pl.pallas_call(
        functools.partial(_inproj_body, groups),
        grid=(bsz, seq // tm),
        in_specs=[row(d), per_batch, per_batch, _resident((1, d)), _resident((d, n_tot))],
        out_specs=[row(w) for _, w, _ in groups],
        out_shape=[jax.ShapeDtypeStruct((bsz, seq, w), dt) for _, w, dt in groups],
        compiler_params=pltpu.CompilerParams(
            dimension_semantics=("arbitrary", "arbitrary"), vmem_limit_bytes=_vmem_limit(est)),
        name="in_projection",
    )(x, sc1, sh1, norm_w.reshape(1, d), w_packed)


def _attn_body(sink_ref, q_ref, kp_ref, kc_ref, vp_ref, vc_ref, bias_ref, o_ref):
    n = pl.program_id(1)
    q = q_ref[0]
    kb = jnp.concatenate([kp_ref[0], kc_ref[0]], axis=0)
    vb = jnp.concatenate([vp_ref[0], vc_ref[0]], axis=0)
    qi = lax.broadcasted_iota(jnp.int32, (WINDOW, 2 * WINDOW), 0)
    kj = lax.broadcasted_iota(jnp.int32, (WINDOW, 2 * WINDOW), 1)
    dist = WINDOW + qi - kj
    valid = (dist >= 0) & (dist < WINDOW) & ((kj >= WINDOW) | (n > 0))
    heads = range(ATTN_Q_HEADS)
    head_cols = lambda x, i: x[:, i * ATTN_HEAD_DIM:(i + 1) * ATTN_HEAD_DIM]
    scores = [jnp.where(valid, _dot_nt(head_cols(q, h), head_cols(kb, h // ATTN_GROUP)) + bias_ref[h], NEG_INF)
              for h in heads]
    maxes = [jnp.maximum(jnp.max(scores[h], axis=-1, keepdims=True), sink_ref[h]) for h in heads]
    probs = [jnp.exp(scores[h] - maxes[h]) for h in heads]
    denoms = [jnp.sum(probs[h], axis=-1, keepdims=True) + jnp.exp(sink_ref[h] - maxes[h]) for h in heads]
    outs = [_dot(probs[h].astype(BF16), head_cols(vb, h // ATTN_GROUP)) / denoms[h] for h in heads]
    o_ref[0] = jnp.concatenate(outs, axis=1).astype(o_ref.dtype)


def _attention(q, k, v, sinks, bias):
    bsz, seq, _ = q.shape
    nb = seq // WINDOW
    cur = lambda width: pl.BlockSpec((1, WINDOW, width), lambda b, n: (b, n, 0))
    prev = lambda width: pl.BlockSpec((1, WINDOW, width), lambda b, n: (b, jnp.maximum(n - 1, 0), 0))
    return pl.pallas_call(
        _attn_body,
        grid=(bsz, nb),
        in_specs=[
            pl.BlockSpec(memory_space=pltpu.SMEM),
            cur(ATTN_Q_DIM), prev(ATTN_KV_DIM), cur(ATTN_KV_DIM), prev(ATTN_KV_DIM), cur(ATTN_KV_DIM),
            _resident((ATTN_Q_HEADS, WINDOW, 2 * WINDOW)),
        ],
        out_specs=cur(ATTN_Q_DIM),
        out_shape=jax.ShapeDtypeStruct((bsz, seq, ATTN_Q_DIM), BF16),
        compiler_params=pltpu.CompilerParams(dimension_semantics=("arbitrary", "arbitrary")),
        name="swa_attention",
    )(sinks.astype(F32), q, k, k, v, v, bias)


def _split3(a):
    hi = a.astype(BF16)
    r = a - hi.astype(F32)
    mid = r.astype(BF16)
    lo = (r - mid.astype(F32)).astype(BF16)
    return hi, mid, lo


def _mm(a, b, dot=_dot):
    return dot(a.astype(BF16), b.astype(BF16))


def _unit_lower_inverses(lowers, ii, jj):
    size = lowers[0].shape[0]
    eye = jnp.where(ii == jj, 1.0, 0.0)
    invs = [eye] * len(lowers)
    s = 1
    while s < size:
        sh = int(math.log2(s))
        joins = (jnp.right_shift(ii, sh + 1) == jnp.right_shift(jj, sh + 1)) & (
            jnp.right_shift(ii, sh) != jnp.right_shift(jj, sh))
        es = [jnp.where(joins, lower, 0.0) for lower in lowers]
        if s == 1:
            invs = [inv - e for inv, e in zip(invs, es)]
        else:
            tmps = [_mm(e, inv) for e, inv in zip(es, invs)]
            invs = [inv - _mm(inv, tmp) for inv, tmp in zip(invs, tmps)]
        s *= 2
    return invs


def _dn_body(qkv_ref, z_ref, bd_ref, cw_ref, alog_ref, dtb_ref, nw_ref, o_ref, state_ref, xpad_ref):
    n_b, rows = qkv_ref.shape[0], qkv_ref.shape[1]
    hd = DN_HEAD_DIM
    sub = V7X_SUBLANES

    @pl.when(pl.program_id(1) == 0)
    def _():
        state_ref[...] = jnp.zeros_like(state_ref)
        xpad_ref[:, 0:sub, :] = jnp.zeros((n_b, sub, xpad_ref.shape[2]), F32)

    ii = lax.broadcasted_iota(jnp.int32, (rows, rows), 0)
    jj = lax.broadcasted_iota(jnp.int32, (rows, rows), 1)
    incl = jj <= ii
    strict = jj < ii
    tri = jnp.where(incl, 1.0, 0.0).astype(BF16)
    nw = nw_ref[...]

    units = [(b, h) for b in range(n_b) for h in range(DN_HEADS)]
    q_l, k_l, rhs_l, gcc_l, eg_l, gend_l, lower_l, intra_l = [], [], [], [], [], [], [], []
    for b in range(n_b):
        x = qkv_ref[b].astype(F32)
        xpad_ref[b, sub:sub + rows, :] = x
        acc = cw_ref[DN_CONV - 1:DN_CONV, :] * x
        for j in range(DN_CONV - 1):
            start = sub - (DN_CONV - 1) + j
            acc = acc + cw_ref[j:j + 1, :] * xpad_ref[b, start:start + rows, :]
        xpad_ref[b, 0:sub, :] = x[rows - sub:rows, :]
        a = _silu(acc)
        bd = bd_ref[b]
        beta_all = _sigmoid(bd)
        g_all = -jnp.exp(alog_ref[...]) * jax.nn.softplus(bd + dtb_ref[...])
        g_hi, g_mid, g_lo = _split3(g_all)
        gc = _dot(tri, g_hi) + _dot(tri, g_mid) + _dot(tri, g_lo)
        gc_t = gc.T
        eg_all = jnp.exp(gc)
        for h in range(DN_HEADS):
            q = a[:, h * hd:(h + 1) * hd]
            k = a[:, DN_DIM + h * hd:DN_DIM + (h + 1) * hd]
            v = a[:, 2 * DN_DIM + h * hd:2 * DN_DIM + (h + 1) * hd]
            q = q * lax.rsqrt(jnp.sum(q * q, axis=-1, keepdims=True) + L2_EPS) * (hd ** -0.5)
            k = k * lax.rsqrt(jnp.sum(k * k, axis=-1, keepdims=True) + L2_EPS)
            beta = beta_all[:, h:h + 1]
            gcc = gc[:, DN_HEADS + h:DN_HEADS + h + 1]
            gcr = gc_t[DN_HEADS + h:DN_HEADS + h + 1, :]
            eg = eg_all[:, DN_HEADS + h:DN_HEADS + h + 1]
            decay = jnp.where(incl, jnp.exp(jnp.where(incl, gcc - gcr, 0.0)), 0.0)
            kb = k * beta
            q_l.append(q); k_l.append(k); gcc_l.append(gcc); eg_l.append(eg)
            gend_l.append(gc[rows - 1:rows, DN_HEADS + h:DN_HEADS + h + 1])
            rhs_l.append(jnp.concatenate([v * beta, kb * eg], axis=1))
            lower_l.append(jnp.where(strict, _mm(kb, k, _dot_nt) * decay, 0.0))
            intra_l.append(jnp.where(incl, _mm(q, k, _dot_nt) * decay, 0.0))
    inv_l = _unit_lower_inverses(lower_l, ii, jj)
    sol_l = [_mm(inv, rhs) for inv, rhs in zip(inv_l, rhs_l)]
    state_l = [state_ref[b, h] for b, h in units]
    vnew_l = [sol[:, :hd] - _mm(sol[:, hd:], state) for sol, state in zip(sol_l, state_l)]
    for i, (b, h) in enumerate(units):
        o = _mm(q_l[i] * eg_l[i], state_l[i]) + _mm(intra_l[i], vnew_l[i])
        k_tail = k_l[i] * jnp.exp(gend_l[i] - gcc_l[i])
        state_ref[b, h] = state_l[i] * jnp.exp(gend_l[i]) + _mm(k_tail, vnew_l[i], _dot_tn)
        y = _rms(o, nw) * _silu(z_ref[b, :, h * hd:(h + 1) * hd].astype(F32))
        o_ref[b, :, h * hd:(h + 1) * hd] = y.astype(o_ref.dtype)


def _deltanet(dqkv, dz, bd, conv_w, a_log, dt_bias, norm_w):
    bsz, seq, _ = dqkv.shape
    rows = min(DN_ROWS, seq)
    n_b = math.gcd(DN_BATCH, bsz)
    lane_row = lambda vec: jnp.pad(vec.astype(F32), (DN_HEADS, BD_PAD - 2 * DN_HEADS)).reshape(1, BD_PAD)
    blk = lambda width: pl.BlockSpec((n_b, rows, width), lambda b, s: (b, s, 0))
    est = n_b * (2 * rows * (3 * DN_DIM * 2 + DN_DIM * 2 + BD_PAD * 4 + DN_DIM * 2) + 12 * rows * 3 * DN_DIM * 4
                 + DN_HEADS * 24 * rows * rows * 4)
    return pl.pallas_call(
        _dn_body,
        grid=(bsz // n_b, seq // rows),
        in_specs=[
            blk(3 * DN_DIM), blk(DN_DIM), blk(BD_PAD),
            _resident((DN_CONV, 3 * DN_DIM)), _resident((1, BD_PAD)), _resident((1, BD_PAD)),
            _resident((1, DN_HEAD_DIM)),
        ],
        out_specs=blk(DN_DIM),
        out_shape=jax.ShapeDtypeStruct((bsz, seq, DN_DIM), BF16),
        scratch_shapes=[
            pltpu.VMEM((n_b, DN_HEADS, DN_HEAD_DIM, DN_HEAD_DIM), F32),
            pltpu.VMEM((n_b, rows + V7X_SUBLANES, 3 * DN_DIM), F32),
        ],
        compiler_params=pltpu.CompilerParams(
            dimension_semantics=("arbitrary", "arbitrary"), vmem_limit_bytes=_vmem_limit(est)),
        name="gated_deltanet",
    )(dqkv, dz, bd, conv_w.astype(F32), lane_row(a_log), lane_row(dt_bias), norm_w.astype(F32).reshape(1, -1))


def _mix_body(x_ref, ya_ref, yd_ref, ga_ref, gd_ref, wab_ref, wdb_ref, wo_ref, npost_ref, g1_ref, npre_ref,
              sc2_ref, sh2_ref, h_ref, u_ref):
    ba = _dot(ya_ref[0], wab_ref[...])
    bdn = _dot(yd_ref[0], wdb_ref[...])
    merged = _sigmoid(ga_ref[0].astype(F32)) * ba + _sigmoid(gd_ref[0].astype(F32)) * bdn
    y = _dot(merged.astype(BF16), wo_ref[...])
    h1 = x_ref[0] + g1_ref[0] * _rms(y, npost_ref[...])
    h_ref[0] = h1
    u_ref[0] = (_rms(h1, npre_ref[...]) * (1.0 + sc2_ref[0]) + sh2_ref[0]).astype(u_ref.dtype)


def _mix_out(x, ya, yd, ga, gd, w_ab, w_db, w_out, norm_post, g1, norm_pre, sc2, sh2):
    bsz, seq, d = x.shape
    tm = min(ROW_TILE, seq)
    row = lambda width: pl.BlockSpec((1, tm, width), lambda b, t: (b, t, 0))
    per_batch = pl.BlockSpec((1, 1, d), lambda b, t: (b, 0, 0))
    est = (2 * tm * d * (4 + 2 + 2 + 4 + 2) + 2 * tm * 2 * ATTN_Q_DIM * 2 + (2 * ATTN_Q_DIM + d) * d * 2
           + 6 * tm * d * 4)
    return pl.pallas_call(
        _mix_body,
        grid=(bsz, seq // tm),
        in_specs=[
            row(d), row(ATTN_Q_DIM), row(DN_DIM), row(d), row(d),
            _resident((ATTN_Q_DIM, d)), _resident((DN_DIM, d)), _resident((d, d)),
            _resident((1, d)), per_batch, _resident((1, d)), per_batch, per_batch,
        ],
        out_specs=[row(d), row(d)],
        out_shape=[jax.ShapeDtypeStruct((bsz, seq, d), F32), jax.ShapeDtypeStruct((bsz, seq, d), BF16)],
        compiler_params=pltpu.CompilerParams(
            dimension_semantics=("arbitrary", "arbitrary"), vmem_limit_bytes=_vmem_limit(est)),
        name="merge_out_projection",
    )(x, ya, yd, ga, gd, w_ab, w_db, w_out, norm_post.reshape(1, d), g1, norm_pre.reshape(1, d), sc2, sh2)


def _ffn_body(u_ref, h_ref, wup_ref, cw_ref, wdn_ref, npost_ref, g2_ref, o_ref, gpad_ref, vpad_ref, gcar_ref,
              vcar_ref, act_ref):
    tm = u_ref.shape[1]
    n_ch = D_FF // FF_CHUNK
    sub = V7X_SUBLANES

    @pl.when(pl.program_id(1) == 0)
    def _():
        gcar_ref[...] = jnp.zeros_like(gcar_ref)
        vcar_ref[...] = jnp.zeros_like(vcar_ref)

    u = u_ref[0]

    def conv_half(pad_ref, car_ref, j, off):
        up = _dot(u, wup_ref[:, off:off + FF_CHUNK])
        pad_ref[0:sub, :] = car_ref[j]
        pad_ref[sub:sub + tm, :] = up
        car_ref[j] = up[tm - sub:tm, :]
        out = cw_ref[FFN_CONV - 1:FFN_CONV, off:off + FF_CHUNK] * up
        for t in range(FFN_CONV - 1):
            start = sub - (FFN_CONV - 1) + t
            out = out + cw_ref[t:t + 1, off:off + FF_CHUNK] * pad_ref[start:start + tm, :]
        return out

    for j in range(n_ch):
        gate = conv_half(gpad_ref, gcar_ref, j, j * FF_CHUNK)
        val = conv_half(vpad_ref, vcar_ref, j, D_FF + j * FF_CHUNK)
        act_ref[:, j * FF_CHUNK:(j + 1) * FF_CHUNK] = (jax.nn.gelu(gate, approximate=True) * val).astype(BF16)

    y = _dot(act_ref[...], wdn_ref[...])
    o_ref[0] = h_ref[0] + g2_ref[0] * _rms(y, npost_ref[...])


def _ffn(u2, h1, w_up, conv_w, w_down, norm_post, g2):
    bsz, seq, d = h1.shape
    tm = min(ROW_TILE, seq)
    n_ch = D_FF // FF_CHUNK
    row = lambda width: pl.BlockSpec((1, tm, width), lambda b, t: (b, t, 0))
    per_batch = pl.BlockSpec((1, 1, d), lambda b, t: (b, 0, 0))
    est = (2 * tm * d * (2 + 4 + 4) + 3 * d * D_FF * 2 + 2 * (tm + 8) * FF_CHUNK * 4 + tm * D_FF * 2
           + 8 * tm * FF_CHUNK * 4 + 2 * tm * d * 4)
    return pl.pallas_call(
        _ffn_body,
        grid=(bsz, seq // tm),
        in_specs=[
            row(d), row(d),
            _resident((d, 2 * D_FF)), _resident((FFN_CONV, 2 * D_FF)), _resident((D_FF, d)),
            _resident((1, d)), per_batch,
        ],
        out_specs=row(d),
        out_shape=jax.ShapeDtypeStruct((bsz, seq, d), F32),
        scratch_shapes=[
            pltpu.VMEM((tm + V7X_SUBLANES, FF_CHUNK), F32),
            pltpu.VMEM((tm + V7X_SUBLANES, FF_CHUNK), F32),
            pltpu.VMEM((n_ch, V7X_SUBLANES, FF_CHUNK), F32),
            pltpu.VMEM((n_ch, V7X_SUBLANES, FF_CHUNK), F32),
            pltpu.VMEM((tm, D_FF), BF16),
        ],
        compiler_params=pltpu.CompilerParams(
            dimension_semantics=("arbitrary", "arbitrary"), vmem_limit_bytes=_vmem_limit(est)),
        name="conv_geglu_ffn",
    )(u2, h1, w_up, conv_w.astype(F32), w_down, norm_post.reshape(1, d), g2)


def _layer(h, c, ada_w, ada_b, norm_mix_pre, norm_mix_post, norm_ffn_pre, norm_ffn_post, w_in, dn_conv_w,
           dn_a_log, dn_dt_bias, dn_norm_w, attn_sinks, bias, w_attn_branch, w_dn_branch, w_out, ffn_w_up,
           ffn_conv_w, ffn_w_down):
    bsz, seq, d = h.shape
    mod = _modulation(c, ada_w, ada_b)
    sh1, sc1, g1, sh2, sc2, g2 = [mod[:, i * d:(i + 1) * d].reshape(bsz, 1, d) for i in range(N_MOD)]
    q, k, v, dqkv, dz, ga, gd, bd = _in_projection(h, sc1, sh1, norm_mix_pre, _pack_w_in(w_in))
    y_attn = _attention(q, k, v, attn_sinks, bias)
    y_dn = _deltanet(dqkv, dz, bd, dn_conv_w, dn_a_log, dn_dt_bias, dn_norm_w)
    h1, u2 = _mix_out(h, y_attn, y_dn, ga, gd, w_attn_branch.astype(BF16), w_dn_branch.astype(BF16),
                      w_out.astype(BF16), norm_mix_post, g1, norm_ffn_pre, sc2, sh2)
    return _ffn(u2, h1, ffn_w_up.astype(BF16), ffn_conv_w, ffn_w_down.astype(BF16), norm_ffn_post, g2)


def kernel(x, c, ada_w, ada_b, norm_mix_pre, norm_mix_post, norm_ffn_pre, norm_ffn_post, w_in, dn_conv_w, dn_a_log, dn_dt_bias, dn_norm_w, attn_sinks, rel_bias, w_attn_branch, w_dn_branch, w_out, ffn_w_up, ffn_conv_w, ffn_w_down):
    bias = _rel_bias_table(rel_bias)
    h = x
    for l in range(ada_w.shape[0]):
        h = _layer(h, c, ada_w[l], ada_b[l], norm_mix_pre[l], norm_mix_post[l], norm_ffn_pre[l],
                   norm_ffn_post[l], w_in[l], dn_conv_w[l], dn_a_log[l], dn_dt_bias[l], dn_norm_w[l],
                   attn_sinks[l], bias, w_attn_branch[l], w_dn_branch[l], w_out[l], ffn_w_up[l],
                   ffn_conv_w[l], ffn_w_down[l])
    return h
```

```python
import functools
import math

import numpy as np
import jax
import jax.numpy as jnp
from jax import lax
from jax.experimental import pallas as pl
from jax.experimental.pallas import tpu as pltpu

ATTN_Q_HEADS = 8
ATTN_KV_HEADS = 2
ATTN_HEAD_DIM = 64
WINDOW = 128
REL_BUCKETS = 32
REL_MAX_DIST = 128
DN_HEADS = 4
DN_HEAD_DIM = 128
DN_CONV = 4
DN_CHUNK = 128
D_FF = 2816
FFN_CONV = 3
RMS_EPS = 1e-6
L2_EPS = 1e-6
N_MOD = 6
NEG_INF = -1e30

ATTN_Q_DIM = ATTN_Q_HEADS * ATTN_HEAD_DIM
ATTN_KV_DIM = ATTN_KV_HEADS * ATTN_HEAD_DIM
DN_DIM = DN_HEADS * DN_HEAD_DIM
ATTN_GROUP = ATTN_Q_HEADS // ATTN_KV_HEADS

V7X_LANES = 128
V7X_SUBLANES = 8
V7X_VMEM_BYTES = 64 * 1024 * 1024

F32 = jnp.float32
BF16 = jnp.bfloat16

ROW_TILE = 512
DN_ROWS = 256
DN_BATCH = 2
FF_CHUNK = 256
BD_PAD = V7X_LANES


def _vmem_limit(nbytes):
    return int(min(V7X_VMEM_BYTES, nbytes + nbytes // 2 + (8 << 20)))


def _resident(shape):
    nd = len(shape)
    return pl.BlockSpec(shape, lambda *_: (0,) * nd, pipeline_mode=pl.Buffered(1))


def _dot(a, b):
    return jnp.dot(a, b, preferred_element_type=F32)


def _dot_nt(a, b):
    return lax.dot_general(a, b, (((1,), (1,)), ((), ())), preferred_element_type=F32)


def _dot_tn(a, b):
    return lax.dot_general(a, b, (((0,), (0,)), ((), ())), preferred_element_type=F32)


def _rms(xf, w):
    return xf * lax.rsqrt(jnp.mean(xf * xf, axis=-1, keepdims=True) + RMS_EPS) * w


def _sigmoid(x):
    return 1.0 / (1.0 + jnp.exp(-x))


def _silu(x):
    return x * _sigmoid(x)


def _mod_body(c_ref, w_ref, b_ref, o_ref):
    ca = _silu(c_ref[...])
    o_ref[...] = _dot(ca.astype(BF16), w_ref[...].astype(BF16)) + b_ref[...]


def _modulation(c, ada_w, ada_b):
    bsz, d = c.shape
    n = ada_w.shape[1]
    tn = d
    return pl.pallas_call(
        _mod_body,
        grid=(n // tn,),
        in_specs=[
            pl.BlockSpec((bsz, d), lambda j: (0, 0)),
            pl.BlockSpec((d, tn), lambda j: (0, j)),
            pl.BlockSpec((1, tn), lambda j: (0, j)),
        ],
        out_specs=pl.BlockSpec((bsz, tn), lambda j: (0, j)),
        out_shape=jax.ShapeDtypeStruct((bsz, n), F32),
        compiler_params=pltpu.CompilerParams(
            dimension_semantics=("arbitrary",), vmem_limit_bytes=_vmem_limit(2 * d * tn * 4)),
        name="adaln_modulation",
    )(c, ada_w, ada_b.reshape(1, n))


def _t5_bucket_band():
    qi = np.arange(WINDOW)[:, None]
    kj = np.arange(2 * WINDOW)[None, :]
    dist = np.maximum(WINDOW + qi - kj, 0)
    max_exact = REL_BUCKETS // 2
    scaled = np.log(np.maximum(dist, 1).astype(np.float32) / np.float32(max_exact)) / np.float32(
        math.log(REL_MAX_DIST / max_exact))
    large = max_exact + (scaled.astype(np.float32) * np.float32(REL_BUCKETS - max_exact)).astype(np.int32)
    large = np.minimum(large, REL_BUCKETS - 1)
    return np.where(dist < max_exact, dist, large).astype(np.int32)


def _bias_body(rb_ref, bucket_ref, o_ref):
    h = pl.program_id(0)
    bucket = bucket_ref[...]
    acc = jnp.zeros(bucket.shape, F32)
    for v in range(REL_BUCKETS):
        acc = jnp.where(bucket == v, rb_ref[v, h], acc)
    o_ref[0] = acc


def _rel_bias_table(rel_bias):
    bucket = jnp.asarray(_t5_bucket_band())
    return pl.pallas_call(
        _bias_body,
        grid=(ATTN_Q_HEADS,),
        in_specs=[
            pl.BlockSpec(memory_space=pltpu.SMEM),
            pl.BlockSpec((WINDOW, 2 * WINDOW), lambda h: (0, 0)),
        ],
        out_specs=pl.BlockSpec((1, WINDOW, 2 * WINDOW), lambda h: (h, 0, 0)),
        out_shape=jax.ShapeDtypeStruct((ATTN_Q_HEADS, WINDOW, 2 * WINDOW), F32),
        compiler_params=pltpu.CompilerParams(dimension_semantics=("arbitrary",)),
        name="rel_bias_table",
    )(rel_bias.astype(F32), bucket)


_IN_GROUPS = (
    ("q", ATTN_Q_DIM, BF16),
    ("k", ATTN_KV_DIM, BF16),
    ("v", ATTN_KV_DIM, BF16),
    ("dqkv", 3 * DN_DIM, BF16),
    ("dz", DN_DIM, BF16),
    ("ga", None, BF16),
    ("gd", None, BF16),
    ("bd", BD_PAD, F32),
)


def _in_groups(d_model):
    return tuple((n, d_model if w is None else w, dt) for n, w, dt in _IN_GROUPS)


def _pack_w_in(w_in):
    d = w_in.shape[0]
    o = 0
    aq = w_in[:, o:o + ATTN_Q_DIM] * (ATTN_HEAD_DIM ** -0.5); o += ATTN_Q_DIM
    ak = w_in[:, o:o + ATTN_KV_DIM]; o += ATTN_KV_DIM
    av = w_in[:, o:o + ATTN_KV_DIM]; o += ATTN_KV_DIM
    dqkv = w_in[:, o:o + 3 * DN_DIM]; o += 3 * DN_DIM
    dz = w_in[:, o:o + DN_DIM]; o += DN_DIM
    bd = w_in[:, o:o + 2 * DN_HEADS]; o += 2 * DN_HEADS
    ga = w_in[:, o:o + d]; o += d
    gd = w_in[:, o:o + d]; o += d
    bd = jnp.pad(bd, ((0, 0), (0, BD_PAD - 2 * DN_HEADS)))
    return jnp.concatenate([aq, ak, av, dqkv, dz, ga, gd, bd], axis=1).astype(BF16)


def _dn_activate(y, part, cw_ref, xpad_ref):
    rows = y.shape[0]
    sub = V7X_SUBLANES
    cols = slice(part * DN_DIM, (part + 1) * DN_DIM)
    xpad_ref[sub:sub + rows, cols] = y
    acc = cw_ref[DN_CONV - 1:DN_CONV, cols] * y
    for j in range(DN_CONV - 1):
        start = sub - (DN_CONV - 1) + j
        acc = acc + cw_ref[j:j + 1, cols] * xpad_ref[start:start + rows, cols]
    xpad_ref[0:sub, cols] = y[rows - sub:rows, :]
    a = _silu(acc)
    if part == 2:
        return a
    heads = []
    for h in range(DN_HEADS):
        t = a[:, h * DN_HEAD_DIM:(h + 1) * DN_HEAD_DIM]
        t = t * lax.rsqrt(jnp.sum(t * t, axis=-1, keepdims=True) + L2_EPS)
        heads.append(t * (DN_HEAD_DIM ** -0.5) if part == 0 else t)
    return jnp.concatenate(heads, axis=1)


def _inproj_body(groups, x_ref, sc_ref, sh_ref, nw_ref, w_ref, cw_ref, *refs):
    out_refs, xpad_ref = refs[:-1], refs[-1]

    @pl.when(pl.program_id(1) == 0)
    def _():
        xpad_ref[0:V7X_SUBLANES, :] = jnp.zeros((V7X_SUBLANES, xpad_ref.shape[1]), F32)

    xf = x_ref[0]
    u = (_rms(xf, nw_ref[...]) * (1.0 + sc_ref[0]) + sh_ref[0]).astype(BF16)
    offs = np.cumsum([0] + [w for _, w, _ in groups])
    order = sorted(range(len(groups)), key=lambda i: groups[i][0] != "dqkv")
    for i in order:
        (name, width, dt), o_ref, off = groups[i], out_refs[i], int(offs[i])
        if name == "dqkv":
            for part in range(3):
                lo = part * DN_DIM
                y = _dot(u, w_ref[:, off + lo:off + lo + DN_DIM])
                o_ref[0, :, lo:lo + DN_DIM] = _dn_activate(y, part, cw_ref, xpad_ref).astype(dt)
        else:
            o_ref[0] = _dot(u, w_ref[:, off:off + width]).astype(dt)


def _in_projection(x, sc1, sh1, norm_w, w_packed, dn_conv_w):
    bsz, seq, d = x.shape
    groups = _in_groups(d)
    n_tot = sum(w for _, w, _ in groups)
    tm = min(ROW_TILE, seq)
    row = lambda width: pl.BlockSpec((1, tm, width), lambda b, t: (b, t, 0))
    per_batch = pl.BlockSpec((1, 1, d), lambda b, t: (b, 0, 0))
    out_bytes = sum(2 * tm * w * jnp.dtype(dt).itemsize for _, w, dt in groups)
    est = 2 * tm * d * 4 + d * n_tot * 2 + out_bytes + 4 * tm * d * 4 + 6 * tm * 3 * DN_DIM * 4
    return pl.pallas_call(
        functools.partial(_inproj_body, groups),
        grid=(bsz, seq // tm),
        in_specs=[row(d), per_batch, per_batch, _resident((1, d)), _resident((d, n_tot)),
                  _resident((DN_CONV, 3 * DN_DIM))],
        out_specs=[row(w) for _, w, _ in groups],
        out_shape=[jax.ShapeDtypeStruct((bsz, seq, w), dt) for _, w, dt in groups],
        scratch_shapes=[pltpu.VMEM((tm + V7X_SUBLANES, 3 * DN_DIM), F32)],
        compiler_params=pltpu.CompilerParams(
            dimension_semantics=("arbitrary", "arbitrary"), vmem_limit_bytes=_vmem_limit(est)),
        name="in_projection",
    )(x, sc1, sh1, norm_w.reshape(1, d), w_packed, dn_conv_w.astype(F32))


def _attn_body(sink_ref, q_ref, kp_ref, kc_ref, vp_ref, vc_ref, bias_ref, o_ref):
    n = pl.program_id(1)
    q = q_ref[0]
    kb = jnp.concatenate([kp_ref[0], kc_ref[0]], axis=0)
    vb = jnp.concatenate([vp_ref[0], vc_ref[0]], axis=0)
    qi = lax.broadcasted_iota(jnp.int32, (WINDOW, 2 * WINDOW), 0)
    kj = lax.broadcasted_iota(jnp.int32, (WINDOW, 2 * WINDOW), 1)
    dist = WINDOW + qi - kj
    valid = (dist >= 0) & (dist < WINDOW) & ((kj >= WINDOW) | (n > 0))
    heads = range(ATTN_Q_HEADS)
    head_cols = lambda x, i: x[:, i * ATTN_HEAD_DIM:(i + 1) * ATTN_HEAD_DIM]
    scores = [jnp.where(valid, _dot_nt(head_cols(q, h), head_cols(kb, h // ATTN_GROUP)) + bias_ref[h], NEG_INF)
              for h in heads]
    maxes = [jnp.maximum(jnp.max(scores[h], axis=-1, keepdims=True), sink_ref[h]) for h in heads]
    probs = [jnp.exp(scores[h] - maxes[h]) for h in heads]
    denoms = [jnp.sum(probs[h], axis=-1, keepdims=True) + jnp.exp(sink_ref[h] - maxes[h]) for h in heads]
    outs = [_dot(probs[h].astype(BF16), head_cols(vb, h // ATTN_GROUP)) / denoms[h] for h in heads]
    o_ref[0] = jnp.concatenate(outs, axis=1).astype(o_ref.dtype)


def _attention(q, k, v, sinks, bias):
    bsz, seq, _ = q.shape
    nb = seq // WINDOW
    cur = lambda width: pl.BlockSpec((1, WINDOW, width), lambda b, n: (b, n, 0))
    prev = lambda width: pl.BlockSpec((1, WINDOW, width), lambda b, n: (b, jnp.maximum(n - 1, 0), 0))
    return pl.pallas_call(
        _attn_body,
        grid=(bsz, nb),
        in_specs=[
            pl.BlockSpec(memory_space=pltpu.SMEM),
            cur(ATTN_Q_DIM), prev(ATTN_KV_DIM), cur(ATTN_KV_DIM), prev(ATTN_KV_DIM), cur(ATTN_KV_DIM),
            _resident((ATTN_Q_HEADS, WINDOW, 2 * WINDOW)),
        ],
        out_specs=cur(ATTN_Q_DIM),
        out_shape=jax.ShapeDtypeStruct((bsz, seq, ATTN_Q_DIM), BF16),
        compiler_params=pltpu.CompilerParams(dimension_semantics=("arbitrary", "arbitrary")),
        name="swa_attention",
    )(sinks.astype(F32), q, k, k, v, v, bias)


def _split3(a):
    hi = a.astype(BF16)
    r = a - hi.astype(F32)
    mid = r.astype(BF16)
    lo = (r - mid.astype(F32)).astype(BF16)
    return hi, mid, lo


def _mm(a, b, dot=_dot):
    return dot(a.astype(BF16), b.astype(BF16))


def _unit_lower_inverses(lowers, ii, jj):
    size = lowers[0].shape[0]
    eye = jnp.where(ii == jj, 1.0, 0.0)
    invs = [eye] * len(lowers)
    s = 1
    while s < size:
        sh = int(math.log2(s))
        joins = (jnp.right_shift(ii, sh + 1) == jnp.right_shift(jj, sh + 1)) & (
            jnp.right_shift(ii, sh) != jnp.right_shift(jj, sh))
        es = [jnp.where(joins, lower, 0.0) for lower in lowers]
        if s == 1:
            invs = [inv - e for inv, e in zip(invs, es)]
        else:
            tmps = [_mm(e, inv) for e, inv in zip(es, invs)]
            invs = [inv - _mm(inv, tmp) for inv, tmp in zip(invs, tmps)]
        s *= 2
    return invs


def _dn_body(qkv_ref, z_ref, bd_ref, alog_ref, dtb_ref, nw_ref, o_ref, state_ref):
    n_b, rows = qkv_ref.shape[0], qkv_ref.shape[1]
    hd = DN_HEAD_DIM

    @pl.when(pl.program_id(1) == 0)
    def _():
        state_ref[...] = jnp.zeros_like(state_ref)

    cc = DN_CHUNK
    n_c = rows // cc
    ri = lax.broadcasted_iota(jnp.int32, (rows, rows), 0)
    ci = lax.broadcasted_iota(jnp.int32, (rows, rows), 1)
    same_chunk = jnp.right_shift(ri, int(math.log2(cc))) == jnp.right_shift(ci, int(math.log2(cc)))
    tri = jnp.where(same_chunk & (ci <= ri), 1.0, 0.0).astype(BF16)
    ii = lax.broadcasted_iota(jnp.int32, (cc, cc), 0)
    jj = lax.broadcasted_iota(jnp.int32, (cc, cc), 1)
    incl = jj <= ii
    strict = jj < ii
    nw = nw_ref[...]

    q_l, k_l, rhs_l, gcc_l, eg_l, gend_l, lower_l, intra_l = [], [], [], [], [], [], [], []
    for b in range(n_b):
        bd = bd_ref[b]
        beta_all = _sigmoid(bd)
        g_all = -jnp.exp(alog_ref[...]) * jax.nn.softplus(bd + dtb_ref[...])
        g_hi, g_mid, g_lo = _split3(g_all)
        gc = _dot(tri, g_hi) + _dot(tri, g_mid) + _dot(tri, g_lo)
        gc_t = gc.T
        eg_all = jnp.exp(gc)
        for h in range(DN_HEADS):
            q_all = qkv_ref[b, :, h * hd:(h + 1) * hd].astype(F32)
            k_all = qkv_ref[b, :, DN_DIM + h * hd:DN_DIM + (h + 1) * hd].astype(F32)
            v_all = qkv_ref[b, :, 2 * DN_DIM + h * hd:2 * DN_DIM + (h + 1) * hd].astype(F32)
            col = DN_HEADS + h
            for c in range(n_c):
                r0 = c * cc
                q = q_all[r0:r0 + cc]
                k = k_all[r0:r0 + cc]
                beta = beta_all[r0:r0 + cc, h:h + 1]
                gcc = gc[r0:r0 + cc, col:col + 1]
                gcr = gc_t[col:col + 1, r0:r0 + cc]
                eg = eg_all[r0:r0 + cc, col:col + 1]
                decay = jnp.where(incl, jnp.exp(jnp.where(incl, gcc - gcr, 0.0)), 0.0)
                kb = k * beta
                q_l.append(q); k_l.append(k); gcc_l.append(gcc); eg_l.append(eg)
                gend_l.append(gc[r0 + cc - 1:r0 + cc, col:col + 1])
                rhs_l.append(jnp.concatenate([v_all[r0:r0 + cc] * beta, kb * eg], axis=1))
                lower_l.append(jnp.where(strict, _mm(kb, k, _dot_nt) * decay, 0.0))
                intra_l.append(jnp.where(incl, _mm(q, k, _dot_nt) * decay, 0.0))
    inv_l = _unit_lower_inverses(lower_l, ii, jj)
    sol_l = [_mm(inv, rhs) for inv, rhs in zip(inv_l, rhs_l)]

    pairs = [(b, h) for b in range(n_b) for h in range(DN_HEADS)]
    unit = lambda b, h, c: (b * DN_HEADS + h) * n_c + c
    state_l = [state_ref[b, h] for b, h in pairs]
    for c in range(n_c):
        r0 = c * cc
        us = [unit(b, h, c) for b, h in pairs]
        vnew_l = [sol_l[u][:, :hd] - _mm(sol_l[u][:, hd:], state) for u, state in zip(us, state_l)]
        o_l = [_mm(q_l[u] * eg_l[u], state) + _mm(intra_l[u], vnew)
               for u, state, vnew in zip(us, state_l, vnew_l)]
        state_l = [state * jnp.exp(gend_l[u]) + _mm(k_l[u] * jnp.exp(gend_l[u] - gcc_l[u]), vnew, _dot_tn)
                   for u, state, vnew in zip(us, state_l, vnew_l)]
        for (b, h), o in zip(pairs, o_l):
            y = _rms(o, nw) * _silu(z_ref[b, r0:r0 + cc, h * hd:(h + 1) * hd].astype(F32))
            o_ref[b, r0:r0 + cc, h * hd:(h + 1) * hd] = y.astype(o_ref.dtype)
    for (b, h), state in zip(pairs, state_l):
        state_ref[b, h] = state


def _deltanet(qkv_act, dz, bd, a_log, dt_bias, norm_w):
    bsz, seq, _ = qkv_act.shape
    rows = min(DN_ROWS, seq)
    n_b = math.gcd(DN_BATCH, bsz)
    lane_row = lambda vec: jnp.pad(vec.astype(F32), (DN_HEADS, BD_PAD - 2 * DN_HEADS)).reshape(1, BD_PAD)
    blk = lambda width: pl.BlockSpec((n_b, rows, width), lambda b, s: (b, s, 0))
    est = n_b * (2 * rows * (3 * DN_DIM * 2 + DN_DIM * 2 + BD_PAD * 4 + DN_DIM * 2) + 6 * rows * 3 * DN_DIM * 4
                 + DN_HEADS * 24 * rows * DN_CHUNK * 4)
    return pl.pallas_call(
        _dn_body,
        grid=(bsz // n_b, seq // rows),
        in_specs=[
            blk(3 * DN_DIM), blk(DN_DIM), blk(BD_PAD),
            _resident((1, BD_PAD)), _resident((1, BD_PAD)), _resident((1, DN_HEAD_DIM)),
        ],
        out_specs=blk(DN_DIM),
        out_shape=jax.ShapeDtypeStruct((bsz, seq, DN_DIM), BF16),
        scratch_shapes=[pltpu.VMEM((n_b, DN_HEADS, DN_HEAD_DIM, DN_HEAD_DIM), F32)],
        compiler_params=pltpu.CompilerParams(
            dimension_semantics=("arbitrary", "arbitrary"), vmem_limit_bytes=_vmem_limit(est)),
        name="gated_deltanet",
    )(qkv_act, dz, bd, lane_row(a_log), lane_row(dt_bias), norm_w.astype(F32).reshape(1, -1))


def _mix_body(x_ref, ya_ref, yd_ref, ga_ref, gd_ref, wab_ref, wdb_ref, wo_ref, npost_ref, g1_ref, npre_ref,
              sc2_ref, sh2_ref, h_ref, u_ref):
    ba = _dot(ya_ref[0], wab_ref[...])
    bdn = _dot(yd_ref[0], wdb_ref[...])
    merged = _sigmoid(ga_ref[0].astype(F32)) * ba + _sigmoid(gd_ref[0].astype(F32)) * bdn
    y = _dot(merged.astype(BF16), wo_ref[...])
    h1 = x_ref[0] + g1_ref[0] * _rms(y, npost_ref[...])
    h_ref[0] = h1
    u_ref[0] = (_rms(h1, npre_ref[...]) * (1.0 + sc2_ref[0]) + sh2_ref[0]).astype(u_ref.dtype)


def _mix_out(x, ya, yd, ga, gd, w_ab, w_db, w_out, norm_post, g1, norm_pre, sc2, sh2):
    bsz, seq, d = x.shape
    tm = min(ROW_TILE, seq)
    row = lambda width: pl.BlockSpec((1, tm, width), lambda b, t: (b, t, 0))
    per_batch = pl.BlockSpec((1, 1, d), lambda b, t: (b, 0, 0))
    est = (2 * tm * d * (4 + 2 + 2 + 4 + 2) + 2 * tm * 2 * ATTN_Q_DIM * 2 + (2 * ATTN_Q_DIM + d) * d * 2
           + 6 * tm * d * 4)
    return pl.pallas_call(
        _mix_body,
        grid=(bsz, seq // tm),
        in_specs=[
            row(d), row(ATTN_Q_DIM), row(DN_DIM), row(d), row(d),
            _resident((ATTN_Q_DIM, d)), _resident((DN_DIM, d)), _resident((d, d)),
            _resident((1, d)), per_batch, _resident((1, d)), per_batch, per_batch,
        ],
        out_specs=[row(d), row(d)],
        out_shape=[jax.ShapeDtypeStruct((bsz, seq, d), F32), jax.ShapeDtypeStruct((bsz, seq, d), BF16)],
        compiler_params=pltpu.CompilerParams(
            dimension_semantics=("arbitrary", "arbitrary"), vmem_limit_bytes=_vmem_limit(est)),
        name="merge_out_projection",
    )(x, ya, yd, ga, gd, w_ab, w_db, w_out, norm_post.reshape(1, d), g1, norm_pre.reshape(1, d), sc2, sh2)


def _ffn_body(u_ref, h_ref, wup_ref, cw_ref, wdn_ref, npost_ref, g2_ref, o_ref, gpad_ref, vpad_ref, gcar_ref,
              vcar_ref, act_ref):
    tm = u_ref.shape[1]
    n_ch = D_FF // FF_CHUNK
    sub = V7X_SUBLANES

    @pl.when(pl.program_id(1) == 0)
    def _():
        gcar_ref[...] = jnp.zeros_like(gcar_ref)
        vcar_ref[...] = jnp.zeros_like(vcar_ref)

    u = u_ref[0]

    def conv_half(pad_ref, car_ref, j, off):
        up = _dot(u, wup_ref[:, off:off + FF_CHUNK])
        pad_ref[0:sub, :] = car_ref[j]
        pad_ref[sub:sub + tm, :] = up
        car_ref[j] = up[tm - sub:tm, :]
        out = cw_ref[FFN_CONV - 1:FFN_CONV, off:off + FF_CHUNK] * up
        for t in range(FFN_CONV - 1):
            start = sub - (FFN_CONV - 1) + t
            out = out + cw_ref[t:t + 1, off:off + FF_CHUNK] * pad_ref[start:start + tm, :]
        return out

    for j in range(n_ch):
        gate = conv_half(gpad_ref, gcar_ref, j, j * FF_CHUNK)
        val = conv_half(vpad_ref, vcar_ref, j, D_FF + j * FF_CHUNK)
        act_ref[:, j * FF_CHUNK:(j + 1) * FF_CHUNK] = (jax.nn.gelu(gate, approximate=True) * val).astype(BF16)

    y = _dot(act_ref[...], wdn_ref[...])
    o_ref[0] = h_ref[0] + g2_ref[0] * _rms(y, npost_ref[...])


def _ffn(u2, h1, w_up, conv_w, w_down, norm_post, g2):
    bsz, seq, d = h1.shape
    tm = min(ROW_TILE, seq)
    n_ch = D_FF // FF_CHUNK
    row = lambda width: pl.BlockSpec((1, tm, width), lambda b, t: (b, t, 0))
    per_batch = pl.BlockSpec((1, 1, d), lambda b, t: (b, 0, 0))
    est = (2 * tm * d * (2 + 4 + 4) + 3 * d * D_FF * 2 + 2 * (tm + 8) * FF_CHUNK * 4 + tm * D_FF * 2
           + 8 * tm * FF_CHUNK * 4 + 2 * tm * d * 4)
    return pl.pallas_call(
        _ffn_body,
        grid=(bsz, seq // tm),
        in_specs=[
            row(d), row(d),
            _resident((d, 2 * D_FF)), _resident((FFN_CONV, 2 * D_FF)), _resident((D_FF, d)),
            _resident((1, d)), per_batch,
        ],
        out_specs=row(d),
        out_shape=jax.ShapeDtypeStruct((bsz, seq, d), F32),
        scratch_shapes=[
            pltpu.VMEM((tm + V7X_SUBLANES, FF_CHUNK), F32),
            pltpu.VMEM((tm + V7X_SUBLANES, FF_CHUNK), F32),
            pltpu.VMEM((n_ch, V7X_SUBLANES, FF_CHUNK), F32),
            pltpu.VMEM((n_ch, V7X_SUBLANES, FF_CHUNK), F32),
            pltpu.VMEM((tm, D_FF), BF16),
        ],
        compiler_params=pltpu.CompilerParams(
            dimension_semantics=("arbitrary", "arbitrary"), vmem_limit_bytes=_vmem_limit(est)),
        name="conv_geglu_ffn",
    )(u2, h1, w_up, conv_w.astype(F32), w_down, norm_post.reshape(1, d), g2)


def _layer(h, c, ada_w, ada_b, norm_mix_pre, norm_mix_post, norm_ffn_pre, norm_ffn_post, w_in, dn_conv_w,
           dn_a_log, dn_dt_bias, dn_norm_w, attn_sinks, bias, w_attn_branch, w_dn_branch, w_out, ffn_w_up,
           ffn_conv_w, ffn_w_down):
    bsz, seq, d = h.shape
    mod = _modulation(c, ada_w, ada_b)
    sh1, sc1, g1, sh2, sc2, g2 = [mod[:, i * d:(i + 1) * d].reshape(bsz, 1, d) for i in range(N_MOD)]
    q, k, v, dqkv, dz, ga, gd, bd = _in_projection(h, sc1, sh1, norm_mix_pre, _pack_w_in(w_in), dn_conv_w)
    y_attn = _attention(q, k, v, attn_sinks, bias)
    y_dn = _deltanet(dqkv, dz, bd, dn_a_log, dn_dt_bias, dn_norm_w)
    h1, u2 = _mix_out(h, y_attn, y_dn, ga, gd, w_attn_branch.astype(BF16), w_dn_branch.astype(BF16),
                      w_out.astype(BF16), norm_mix_post, g1, norm_ffn_pre, sc2, sh2)
    return _ffn(u2, h1, ffn_w_up.astype(BF16), ffn_conv_w, ffn_w_down.astype(BF16), norm_ffn_post, g2)


def kernel(x, c, ada_w, ada_b, norm_mix_pre, norm_mix_post, norm_ffn_pre, norm_ffn_post, w_in, dn_conv_w, dn_a_log, dn_dt_bias, dn_norm_w, attn_sinks, rel_bias, w_attn_branch, w_dn_branch, w_out, ffn_w_up, ffn_conv_w, ffn_w_down):
    bias = _rel_bias_table(rel_bias)
    h = x
    for l in range(ada_w.shape[0]):
        h = _layer(h, c, ada_w[l], ada_b[l], norm_mix_pre[l], norm_mix_post[l], norm_ffn_pre[l],
                   norm_ffn_post[l], w_in[l], dn_conv_w[l], dn_a_log[l], dn_dt_bias[l], dn_norm_w[l],
                   attn_sinks[l], bias, w_attn_branch[l], w_dn_branch[l], w_out[l], ffn_w_up[l],
                   ffn_conv_w[l], ffn_w_down[l])
    return h
```

```python
import functools
import math

import numpy as np
import jax
import jax.numpy as jnp
from jax import lax
from jax.experimental import pallas as pl
from jax.experimental.pallas import tpu as pltpu

ATTN_Q_HEADS = 8
ATTN_KV_HEADS = 2
ATTN_HEAD_DIM = 64
WINDOW = 128
REL_BUCKETS = 32
REL_MAX_DIST = 128
DN_HEADS = 4
DN_HEAD_DIM = 128
DN_CONV = 4
DN_CHUNK = 128
D_FF = 2816
FFN_CONV = 3
RMS_EPS = 1e-6
L2_EPS = 1e-6
N_MOD = 6
NEG_INF = -1e30

ATTN_Q_DIM = ATTN_Q_HEADS * ATTN_HEAD_DIM
ATTN_KV_DIM = ATTN_KV_HEADS * ATTN_HEAD_DIM
DN_DIM = DN_HEADS * DN_HEAD_DIM
ATTN_GROUP = ATTN_Q_HEADS // ATTN_KV_HEADS

V7X_LANES = 128
V7X_SUBLANES = 8
V7X_VMEM_BYTES = 64 * 1024 * 1024

F32 = jnp.float32
BF16 = jnp.bfloat16

ROW_TILE = 512
IN_ROW_TILE = 512
SPLIT_ROWS = 2
ATTN_BLOCKS = 2
DN_ROWS = 256
DN_BATCH = 2
FF_CHUNK = 256
BD_PAD = V7X_LANES


def _vmem_limit(nbytes):
    return int(min(V7X_VMEM_BYTES, nbytes + nbytes // 2 + (8 << 20)))


def _resident(shape):
    nd = len(shape)
    return pl.BlockSpec(shape, lambda *_: (0,) * nd, pipeline_mode=pl.Buffered(1))


def _dot(a, b):
    return jnp.dot(a, b, preferred_element_type=F32)


def _dot_nt(a, b):
    return lax.dot_general(a, b, (((1,), (1,)), ((), ())), preferred_element_type=F32)


def _dot_tn(a, b):
    return lax.dot_general(a, b, (((0,), (0,)), ((), ())), preferred_element_type=F32)


def _rms(xf, w):
    return xf * lax.rsqrt(jnp.mean(xf * xf, axis=-1, keepdims=True) + RMS_EPS) * w


def _sigmoid(x):
    return 1.0 / (1.0 + jnp.exp(-x))


def _silu(x):
    return x * _sigmoid(x)


def _mod_body(c_ref, w_ref, b_ref, o_ref):
    ca = _silu(c_ref[...])
    o_ref[...] = _dot(ca.astype(BF16), w_ref[...].astype(BF16)) + b_ref[...]


def _modulation(c, ada_w, ada_b):
    bsz, d = c.shape
    n = ada_w.shape[1]
    tn = d
    return pl.pallas_call(
        _mod_body,
        grid=(n // tn,),
        in_specs=[
            pl.BlockSpec((bsz, d), lambda j: (0, 0)),
            pl.BlockSpec((d, tn), lambda j: (0, j)),
            pl.BlockSpec((1, tn), lambda j: (0, j)),
        ],
        out_specs=pl.BlockSpec((bsz, tn), lambda j: (0, j)),
        out_shape=jax.ShapeDtypeStruct((bsz, n), F32),
        compiler_params=pltpu.CompilerParams(
            dimension_semantics=("arbitrary",), vmem_limit_bytes=_vmem_limit(2 * d * tn * 4)),
        name="adaln_modulation",
    )(c, ada_w, ada_b.reshape(1, n))


def _t5_bucket_band():
    qi = np.arange(WINDOW)[:, None]
    kj = np.arange(2 * WINDOW)[None, :]
    dist = np.maximum(WINDOW + qi - kj, 0)
    max_exact = REL_BUCKETS // 2
    scaled = np.log(np.maximum(dist, 1).astype(np.float32) / np.float32(max_exact)) / np.float32(
        math.log(REL_MAX_DIST / max_exact))
    large = max_exact + (scaled.astype(np.float32) * np.float32(REL_BUCKETS - max_exact)).astype(np.int32)
    large = np.minimum(large, REL_BUCKETS - 1)
    return np.where(dist < max_exact, dist, large).astype(np.int32)


def _bias_body(rb_ref, bucket_ref, o_ref):
    h = pl.program_id(0)
    bucket = bucket_ref[...]
    acc = jnp.zeros(bucket.shape, F32)
    for v in range(REL_BUCKETS):
        acc = jnp.where(bucket == v, rb_ref[v, h], acc)
    o_ref[0] = acc


def _rel_bias_table(rel_bias):
    bucket = jnp.asarray(_t5_bucket_band())
    return pl.pallas_call(
        _bias_body,
        grid=(ATTN_Q_HEADS,),
        in_specs=[
            pl.BlockSpec(memory_space=pltpu.SMEM),
            pl.BlockSpec((WINDOW, 2 * WINDOW), lambda h: (0, 0)),
        ],
        out_specs=pl.BlockSpec((1, WINDOW, 2 * WINDOW), lambda h: (h, 0, 0)),
        out_shape=jax.ShapeDtypeStruct((ATTN_Q_HEADS, WINDOW, 2 * WINDOW), F32),
        compiler_params=pltpu.CompilerParams(dimension_semantics=("arbitrary",)),
        name="rel_bias_table",
    )(rel_bias.astype(F32), bucket)


_IN_GROUPS = (
    ("q", ATTN_Q_DIM, BF16),
    ("k", ATTN_KV_DIM, BF16),
    ("v", ATTN_KV_DIM, BF16),
    ("dqkv", 3 * DN_DIM, BF16),
    ("dz", DN_DIM, BF16),
    ("ga", None, BF16),
    ("gd", None, BF16),
    ("bd", BD_PAD, F32),
)


def _in_groups(d_model):
    return tuple((n, d_model if w is None else w, dt) for n, w, dt in _IN_GROUPS)


def _pack_w_in(w_in):
    d = w_in.shape[0]
    o = 0
    aq = w_in[:, o:o + ATTN_Q_DIM] * (ATTN_HEAD_DIM ** -0.5); o += ATTN_Q_DIM
    ak = w_in[:, o:o + ATTN_KV_DIM]; o += ATTN_KV_DIM
    av = w_in[:, o:o + ATTN_KV_DIM]; o += ATTN_KV_DIM
    dqkv = w_in[:, o:o + 3 * DN_DIM]; o += 3 * DN_DIM
    dz = w_in[:, o:o + DN_DIM]; o += DN_DIM
    bd = w_in[:, o:o + 2 * DN_HEADS]; o += 2 * DN_HEADS
    ga = w_in[:, o:o + d]; o += d
    gd = w_in[:, o:o + d]; o += d
    bd = jnp.pad(bd, ((0, 0), (0, BD_PAD - 2 * DN_HEADS)))
    return jnp.concatenate([aq, ak, av, dqkv, dz, ga, gd, bd], axis=1).astype(BF16)


def _dn_activate(y, part, cw_ref, xpad_ref, o_ref):
    rows = y.shape[0]
    sub = V7X_SUBLANES
    cols = slice(part * DN_DIM, (part + 1) * DN_DIM)
    xpad_ref[sub:sub + rows, cols] = y
    acc = cw_ref[DN_CONV - 1:DN_CONV, cols] * y
    for j in range(DN_CONV - 1):
        start = sub - (DN_CONV - 1) + j
        acc = acc + cw_ref[j:j + 1, cols] * xpad_ref[start:start + rows, cols]
    xpad_ref[0:sub, cols] = y[rows - sub:rows, :]
    a = _silu(acc)
    if part < 2:
        heads = []
        for h in range(DN_HEADS):
            t = a[:, h * DN_HEAD_DIM:(h + 1) * DN_HEAD_DIM]
            t = t * lax.rsqrt(jnp.sum(t * t, axis=-1, keepdims=True) + L2_EPS)
            heads.append(t * (DN_HEAD_DIM ** -0.5) if part == 0 else t)
        a = jnp.concatenate(heads, axis=1)
    o_ref[0, :, cols] = a.astype(o_ref.dtype)


def _inproj_body(groups, x_ref, sc_ref, sh_ref, nw_ref, w_ref, cw_ref, *refs):
    out_refs, xpad_ref = refs[:-1], refs[-1]

    @pl.when(pl.program_id(1) == 0)
    def _():
        xpad_ref[0:V7X_SUBLANES, :] = jnp.zeros((V7X_SUBLANES, xpad_ref.shape[1]), F32)

    xf = x_ref[0]
    u = (_rms(xf, nw_ref[...]) * (1.0 + sc_ref[0]) + sh_ref[0]).astype(BF16)
    offs = np.cumsum([0] + [w for _, w, _ in groups])
    order = sorted(range(len(groups)), key=lambda i: groups[i][0] != "dqkv")
    for i in order:
        (name, width, dt), o_ref, off = groups[i], out_refs[i], int(offs[i])
        if name == "dqkv":
            for part in range(3):
                lo = part * DN_DIM
                _dn_activate(_dot(u, w_ref[:, off + lo:off + lo + DN_DIM]), part, cw_ref, xpad_ref, o_ref)
        else:
            o_ref[0] = _dot(u, w_ref[:, off:off + width]).astype(dt)


def _in_projection(x, sc1, sh1, norm_w, w_packed, dn_conv_w):
    bsz, seq, d = x.shape
    groups = _in_groups(d)
    n_tot = sum(w for _, w, _ in groups)
    tm = min(IN_ROW_TILE, seq)
    row = lambda width: pl.BlockSpec((1, tm, width), lambda b, t: (b, t, 0))
    per_batch = pl.BlockSpec((1, 1, d), lambda b, t: (b, 0, 0))
    out_bytes = sum(2 * tm * w * jnp.dtype(dt).itemsize for _, w, dt in groups)
    est = 2 * tm * d * 4 + d * n_tot * 2 + out_bytes + 4 * tm * d * 4 + 6 * tm * 3 * DN_DIM * 4
    return pl.pallas_call(
        functools.partial(_inproj_body, groups),
        grid=(bsz, seq // tm),
        in_specs=[row(d), per_batch, per_batch, _resident((1, d)), _resident((d, n_tot)),
                  _resident((DN_CONV, 3 * DN_DIM))],
        out_specs=[row(w) for _, w, _ in groups],
        out_shape=[jax.ShapeDtypeStruct((bsz, seq, w), dt) for _, w, dt in groups],
        scratch_shapes=[pltpu.VMEM((tm + V7X_SUBLANES, 3 * DN_DIM), F32)],
        compiler_params=pltpu.CompilerParams(
            dimension_semantics=("arbitrary", "arbitrary"), vmem_limit_bytes=_vmem_limit(est)),
        name="in_projection",
    )(x, sc1, sh1, norm_w.reshape(1, d), w_packed, dn_conv_w.astype(F32))


def _attn_body(sink_ref, q_ref, kp_ref, kc_ref, vp_ref, vc_ref, bias_ref, o_ref):
    n = pl.program_id(1)
    n_q = q_ref.shape[1] // WINDOW
    k_all = jnp.concatenate([kp_ref[0], kc_ref[0]], axis=0)
    v_all = jnp.concatenate([vp_ref[0], vc_ref[0]], axis=0)
    qi = lax.broadcasted_iota(jnp.int32, (WINDOW, 2 * WINDOW), 0)
    kj = lax.broadcasted_iota(jnp.int32, (WINDOW, 2 * WINDOW), 1)
    dist = WINDOW + qi - kj
    in_band = (dist >= 0) & (dist < WINDOW)
    first_valid = in_band & ((kj >= WINDOW) | (n > 0))
    units = [(j, h) for j in range(n_q) for h in range(ATTN_Q_HEADS)]
    head_cols = lambda x, i: x[:, i * ATTN_HEAD_DIM:(i + 1) * ATTN_HEAD_DIM]
    band = lambda x, j: x[j * WINDOW:(j + 2) * WINDOW]
    scores = [jnp.where(first_valid if j == 0 else in_band,
                        _dot_nt(head_cols(q_ref[0, j * WINDOW:(j + 1) * WINDOW, :], h),
                                head_cols(band(k_all, j), h // ATTN_GROUP)) + bias_ref[h], NEG_INF)
              for j, h in units]
    maxes = [jnp.maximum(jnp.max(s, axis=-1, keepdims=True), sink_ref[h]) for s, (_, h) in zip(scores, units)]
    probs = [jnp.exp(s - m) for s, m in zip(scores, maxes)]
    denoms = [jnp.sum(p, axis=-1, keepdims=True) + jnp.exp(sink_ref[h] - m)
              for p, m, (_, h) in zip(probs, maxes, units)]
    outs = [_dot(p.astype(BF16), head_cols(band(v_all, j), h // ATTN_GROUP)) / d
            for p, d, (j, h) in zip(probs, denoms, units)]
    for j in range(n_q):
        o_ref[0, j * WINDOW:(j + 1) * WINDOW, :] = jnp.concatenate(
            outs[j * ATTN_Q_HEADS:(j + 1) * ATTN_Q_HEADS], axis=1).astype(o_ref.dtype)


def _attention(q, k, v, sinks, bias):
    bsz, seq, _ = q.shape
    n_q = math.gcd(ATTN_BLOCKS, seq // WINDOW)
    cur = lambda width: pl.BlockSpec((1, n_q * WINDOW, width), lambda b, n: (b, n, 0))
    prev = lambda width: pl.BlockSpec((1, WINDOW, width), lambda b, n: (b, jnp.maximum(n * n_q - 1, 0), 0))
    return pl.pallas_call(
        _attn_body,
        grid=(bsz, seq // (n_q * WINDOW)),
        in_specs=[
            pl.BlockSpec(memory_space=pltpu.SMEM),
            cur(ATTN_Q_DIM), prev(ATTN_KV_DIM), cur(ATTN_KV_DIM), prev(ATTN_KV_DIM), cur(ATTN_KV_DIM),
            _resident((ATTN_Q_HEADS, WINDOW, 2 * WINDOW)),
        ],
        out_specs=cur(ATTN_Q_DIM),
        out_shape=jax.ShapeDtypeStruct((bsz, seq, ATTN_Q_DIM), BF16),
        compiler_params=pltpu.CompilerParams(dimension_semantics=("arbitrary", "arbitrary")),
        name="swa_attention",
    )(sinks.astype(F32), q, k, k, v, v, bias)


def _split3(a):
    hi = a.astype(BF16)
    r = a - hi.astype(F32)
    mid = r.astype(BF16)
    lo = (r - mid.astype(F32)).astype(BF16)
    return hi, mid, lo


def _mm(a, b, dot=_dot):
    return dot(a.astype(BF16), b.astype(BF16))


def _unit_lower_inverses(lowers, ii, jj):
    size = lowers[0].shape[0]
    eye = jnp.where(ii == jj, 1.0, 0.0)
    invs = [eye] * len(lowers)
    s = 1
    while s < size:
        sh = int(math.log2(s))
        joins = (jnp.right_shift(ii, sh + 1) == jnp.right_shift(jj, sh + 1)) & (
            jnp.right_shift(ii, sh) != jnp.right_shift(jj, sh))
        es = [jnp.where(joins, lower, 0.0) for lower in lowers]
        if s == 1:
            invs = [inv - e for inv, e in zip(invs, es)]
        else:
            tmps = [_mm(e, inv) for e, inv in zip(es, invs)]
            invs = [inv - _mm(inv, tmp) for inv, tmp in zip(invs, tmps)]
        s *= 2
    return invs


def _dn_body(qkv_ref, z_ref, bd_ref, alog_ref, dtb_ref, nw_ref, o_ref, state_ref):
    n_b, rows = qkv_ref.shape[0], qkv_ref.shape[1]
    hd = DN_HEAD_DIM

    @pl.when(pl.program_id(1) == 0)
    def _():
        state_ref[...] = jnp.zeros_like(state_ref)

    cc = DN_CHUNK
    n_c = rows // cc
    ri = lax.broadcasted_iota(jnp.int32, (rows, rows), 0)
    ci = lax.broadcasted_iota(jnp.int32, (rows, rows), 1)
    same_chunk = jnp.right_shift(ri, int(math.log2(cc))) == jnp.right_shift(ci, int(math.log2(cc)))
    tri = jnp.where(same_chunk & (ci <= ri), 1.0, 0.0).astype(BF16)
    ii = lax.broadcasted_iota(jnp.int32, (cc, cc), 0)
    jj = lax.broadcasted_iota(jnp.int32, (cc, cc), 1)
    incl = jj <= ii
    strict = jj < ii
    nw = nw_ref[...]

    q_l, k_l, rhs_l, gcc_l, eg_l, gend_l, lower_l, intra_l = [], [], [], [], [], [], [], []
    for b in range(n_b):
        bd = bd_ref[b]
        beta_all = _sigmoid(bd)
        g_all = -jnp.exp(alog_ref[...]) * jax.nn.softplus(bd + dtb_ref[...])
        g_hi, g_mid, g_lo = _split3(g_all)
        gc = _dot(tri, g_hi) + _dot(tri, g_mid) + _dot(tri, g_lo)
        gc_t = gc.T
        eg_all = jnp.exp(gc)
        for h in range(DN_HEADS):
            q_all = qkv_ref[b, :, h * hd:(h + 1) * hd].astype(F32)
            k_all = qkv_ref[b, :, DN_DIM + h * hd:DN_DIM + (h + 1) * hd].astype(F32)
            v_all = qkv_ref[b, :, 2 * DN_DIM + h * hd:2 * DN_DIM + (h + 1) * hd].astype(F32)
            col = DN_HEADS + h
            for c in range(n_c):
                r0 = c * cc
                q = q_all[r0:r0 + cc]
                k = k_all[r0:r0 + cc]
                beta = beta_all[r0:r0 + cc, h:h + 1]
                gcc = gc[r0:r0 + cc, col:col + 1]
                gcr = gc_t[col:col + 1, r0:r0 + cc]
                eg = eg_all[r0:r0 + cc, col:col + 1]
                decay = jnp.where(incl, jnp.exp(jnp.where(incl, gcc - gcr, 0.0)), 0.0)
                kb = k * beta
                q_l.append(q); k_l.append(k); gcc_l.append(gcc); eg_l.append(eg)
                gend_l.append(gc[r0 + cc - 1:r0 + cc, col:col + 1])
                rhs_l.append(jnp.concatenate([v_all[r0:r0 + cc] * beta, kb * eg], axis=1))
                lower_l.append(jnp.where(strict, _mm(kb, k, _dot_nt) * decay, 0.0))
                intra_l.append(jnp.where(incl, _mm(q, k, _dot_nt) * decay, 0.0))
    inv_l = _unit_lower_inverses(lower_l, ii, jj)
    sol_l = [_mm(inv, rhs) for inv, rhs in zip(inv_l, rhs_l)]

    pairs = [(b, h) for b in range(n_b) for h in range(DN_HEADS)]
    unit = lambda b, h, c: (b * DN_HEADS + h) * n_c + c
    state_l = [state_ref[b, h] for b, h in pairs]
    for c in range(n_c):
        r0 = c * cc
        us = [unit(b, h, c) for b, h in pairs]
        vnew_l = [sol_l[u][:, :hd] - _mm(sol_l[u][:, hd:], state) for u, state in zip(us, state_l)]
        o_l = [_mm(q_l[u] * eg_l[u], state) + _mm(intra_l[u], vnew)
               for u, state, vnew in zip(us, state_l, vnew_l)]
        state_l = [state * jnp.exp(gend_l[u]) + _mm(k_l[u] * jnp.exp(gend_l[u] - gcc_l[u]), vnew, _dot_tn)
                   for u, state, vnew in zip(us, state_l, vnew_l)]
        for (b, h), o in zip(pairs, o_l):
            y = _rms(o, nw) * _silu(z_ref[b, r0:r0 + cc, h * hd:(h + 1) * hd].astype(F32))
            o_ref[b, r0:r0 + cc, h * hd:(h + 1) * hd] = y.astype(o_ref.dtype)
    for (b, h), state in zip(pairs, state_l):
        state_ref[b, h] = state


def _deltanet(qkv_act, dz, bd, a_log, dt_bias, norm_w):
    bsz, seq, _ = qkv_act.shape
    rows = min(DN_ROWS, seq)
    n_b = math.gcd(DN_BATCH, bsz)
    lane_row = lambda vec: jnp.pad(vec.astype(F32), (DN_HEADS, BD_PAD - 2 * DN_HEADS)).reshape(1, BD_PAD)
    blk = lambda width: pl.BlockSpec((n_b, rows, width), lambda b, s: (b, s, 0))
    est = n_b * (2 * rows * (3 * DN_DIM * 2 + DN_DIM * 2 + BD_PAD * 4 + DN_DIM * 2) + 6 * rows * 3 * DN_DIM * 4
                 + DN_HEADS * 24 * rows * DN_CHUNK * 4)
    return pl.pallas_call(
        _dn_body,
        grid=(bsz // n_b, seq // rows),
        in_specs=[
            blk(3 * DN_DIM), blk(DN_DIM), blk(BD_PAD),
            _resident((1, BD_PAD)), _resident((1, BD_PAD)), _resident((1, DN_HEAD_DIM)),
        ],
        out_specs=blk(DN_DIM),
        out_shape=jax.ShapeDtypeStruct((bsz, seq, DN_DIM), BF16),
        scratch_shapes=[pltpu.VMEM((n_b, DN_HEADS, DN_HEAD_DIM, DN_HEAD_DIM), F32)],
        compiler_params=pltpu.CompilerParams(
            dimension_semantics=("arbitrary", "arbitrary"), vmem_limit_bytes=_vmem_limit(est)),
        name="gated_deltanet",
    )(qkv_act, dz, bd, lane_row(a_log), lane_row(dt_bias), norm_w.astype(F32).reshape(1, -1))


def _mix_body(x_ref, ya_ref, yd_ref, ga_ref, gd_ref, wab_ref, wdb_ref, wo_ref, npost_ref, g1_ref, npre_ref,
              sc2_ref, sh2_ref, h_ref, u_ref):
    tm = x_ref.shape[1]
    half = tm // SPLIT_ROWS
    groups = [slice(r0, r0 + half) for r0 in range(0, tm, half)]
    branch = [(_dot(ya_ref[0, rs, :], wab_ref[...]), _dot(yd_ref[0, rs, :], wdb_ref[...])) for rs in groups]
    ys = []
    for rs, (ba, bdn) in zip(groups, branch):
        merged = (_sigmoid(ga_ref[0, rs, :].astype(F32)) * ba
                  + _sigmoid(gd_ref[0, rs, :].astype(F32)) * bdn)
        ys.append(_dot(merged.astype(BF16), wo_ref[...]))
    for rs, y in zip(groups, ys):
        h1 = x_ref[0, rs, :] + g1_ref[0] * _rms(y, npost_ref[...])
        h_ref[0, rs, :] = h1
        u_ref[0, rs, :] = (_rms(h1, npre_ref[...]) * (1.0 + sc2_ref[0]) + sh2_ref[0]).astype(u_ref.dtype)


def _mix_out(x, ya, yd, ga, gd, w_ab, w_db, w_out, norm_post, g1, norm_pre, sc2, sh2):
    bsz, seq, d = x.shape
    tm = min(ROW_TILE, seq)
    row = lambda width: pl.BlockSpec((1, tm, width), lambda b, t: (b, t, 0))
    per_batch = pl.BlockSpec((1, 1, d), lambda b, t: (b, 0, 0))
    est = (2 * tm * d * (4 + 2 + 2 + 4 + 2) + 2 * tm * 2 * ATTN_Q_DIM * 2 + (2 * ATTN_Q_DIM + d) * d * 2
           + 6 * tm * d * 4)
    return pl.pallas_call(
        _mix_body,
        grid=(bsz, seq // tm),
        in_specs=[
            row(d), row(ATTN_Q_DIM), row(DN_DIM), row(d), row(d),
            _resident((ATTN_Q_DIM, d)), _resident((DN_DIM, d)), _resident((d, d)),
            _resident((1, d)), per_batch, _resident((1, d)), per_batch, per_batch,
        ],
        out_specs=[row(d), row(d)],
        out_shape=[jax.ShapeDtypeStruct((bsz, seq, d), F32), jax.ShapeDtypeStruct((bsz, seq, d), BF16)],
        compiler_params=pltpu.CompilerParams(
            dimension_semantics=("arbitrary", "arbitrary"), vmem_limit_bytes=_vmem_limit(est)),
        name="merge_out_projection",
    )(x, ya, yd, ga, gd, w_ab, w_db, w_out, norm_post.reshape(1, d), g1, norm_pre.reshape(1, d), sc2, sh2)


def _ffn_body(u_ref, h_ref, wup_ref, cw_ref, wdn_ref, npost_ref, g2_ref, o_ref, gpad_ref, vpad_ref, gcar_ref,
              vcar_ref, act_ref):
    tm = u_ref.shape[1]
    n_ch = D_FF // FF_CHUNK
    sub = V7X_SUBLANES

    @pl.when(pl.program_id(1) == 0)
    def _():
        gcar_ref[...] = jnp.zeros_like(gcar_ref)
        vcar_ref[...] = jnp.zeros_like(vcar_ref)

    u = u_ref[0]

    def conv_half(pad_ref, car_ref, j, off):
        up = _dot(u, wup_ref[:, off:off + FF_CHUNK])
        pad_ref[0:sub, :] = car_ref[j]
        pad_ref[sub:sub + tm, :] = up
        car_ref[j] = up[tm - sub:tm, :]
        out = cw_ref[FFN_CONV - 1:FFN_CONV, off:off + FF_CHUNK] * up
        for t in range(FFN_CONV - 1):
            start = sub - (FFN_CONV - 1) + t
            out = out + cw_ref[t:t + 1, off:off + FF_CHUNK] * pad_ref[start:start + tm, :]
        return out

    for j in range(n_ch):
        gate = conv_half(gpad_ref, gcar_ref, j, j * FF_CHUNK)
        val = conv_half(vpad_ref, vcar_ref, j, D_FF + j * FF_CHUNK)
        act_ref[:, j * FF_CHUNK:(j + 1) * FF_CHUNK] = (jax.nn.gelu(gate, approximate=True) * val).astype(BF16)

    y = _dot(act_ref[...], wdn_ref[...])
    o_ref[0] = h_ref[0] + g2_ref[0] * _rms(y, npost_ref[...])


def _ffn(u2, h1, w_up, conv_w, w_down, norm_post, g2):
    bsz, seq, d = h1.shape
    tm = min(ROW_TILE, seq)
    n_ch = D_FF // FF_CHUNK
    row = lambda width: pl.BlockSpec((1, tm, width), lambda b, t: (b, t, 0))
    per_batch = pl.BlockSpec((1, 1, d), lambda b, t: (b, 0, 0))
    est = (2 * tm * d * (2 + 4 + 4) + 3 * d * D_FF * 2 + 2 * (tm + 8) * FF_CHUNK * 4 + tm * D_FF * 2
           + 8 * tm * FF_CHUNK * 4 + 2 * tm * d * 4)
    return pl.pallas_call(
        _ffn_body,
        grid=(bsz, seq // tm),
        in_specs=[
            row(d), row(d),
            _resident((d, 2 * D_FF)), _resident((FFN_CONV, 2 * D_FF)), _resident((D_FF, d)),
            _resident((1, d)), per_batch,
        ],
        out_specs=row(d),
        out_shape=jax.ShapeDtypeStruct((bsz, seq, d), F32),
        scratch_shapes=[
            pltpu.VMEM((tm + V7X_SUBLANES, FF_CHUNK), F32),
            pltpu.VMEM((tm + V7X_SUBLANES, FF_CHUNK), F32),
            pltpu.VMEM((n_ch, V7X_SUBLANES, FF_CHUNK), F32),
            pltpu.VMEM((n_ch, V7X_SUBLANES, FF_CHUNK), F32),
            pltpu.VMEM((tm, D_FF), BF16),
        ],
        compiler_params=pltpu.CompilerParams(
            dimension_semantics=("arbitrary", "arbitrary"), vmem_limit_bytes=_vmem_limit(est)),
        name="conv_geglu_ffn",
    )(u2, h1, w_up, conv_w.astype(F32), w_down, norm_post.reshape(1, d), g2)


def _layer(h, c, ada_w, ada_b, norm_mix_pre, norm_mix_post, norm_ffn_pre, norm_ffn_post, w_in, dn_conv_w,
           dn_a_log, dn_dt_bias, dn_norm_w, attn_sinks, bias, w_attn_branch, w_dn_branch, w_out, ffn_w_up,
           ffn_conv_w, ffn_w_down):
    bsz, seq, d = h.shape
    mod = _modulation(c, ada_w, ada_b)
    sh1, sc1, g1, sh2, sc2, g2 = [mod[:, i * d:(i + 1) * d].reshape(bsz, 1, d) for i in range(N_MOD)]
    q, k, v, dqkv, dz, ga, gd, bd = _in_projection(h, sc1, sh1, norm_mix_pre, _pack_w_in(w_in), dn_conv_w)
    y_attn = _attention(q, k, v, attn_sinks, bias)
    y_dn = _deltanet(dqkv, dz, bd, dn_a_log, dn_dt_bias, dn_norm_w)
    h1, u2 = _mix_out(h, y_attn, y_dn, ga, gd, w_attn_branch.astype(BF16), w_dn_branch.astype(BF16),
                      w_out.astype(BF16), norm_mix_post, g1, norm_ffn_pre, sc2, sh2)
    return _ffn(u2, h1, ffn_w_up.astype(BF16), ffn_conv_w, ffn_w_down.astype(BF16), norm_ffn_post, g2)


def kernel(x, c, ada_w, ada_b, norm_mix_pre, norm_mix_post, norm_ffn_pre, norm_ffn_post, w_in, dn_conv_w, dn_a_log, dn_dt_bias, dn_norm_w, attn_sinks, rel_bias, w_attn_branch, w_dn_branch, w_out, ffn_w_up, ffn_conv_w, ffn_w_down):
    bias = _rel_bias_table(rel_bias)
    h = x
    for l in range(ada_w.shape[0]):
        h = _layer(h, c, ada_w[l], ada_b[l], norm_mix_pre[l], norm_mix_post[l], norm_ffn_pre[l],
                   norm_ffn_post[l], w_in[l], dn_conv_w[l], dn_a_log[l], dn_dt_bias[l], dn_norm_w[l],
                   attn_sinks[l], bias, w_attn_branch[l], w_dn_branch[l], w_out[l], ffn_w_up[l],
                   ffn_conv_w[l], ffn_w_down[l])
    return h
```

```python
import functools
import math

import numpy as np
import jax
import jax.numpy as jnp
from jax import lax
from jax.experimental import pallas as pl
from jax.experimental.pallas import tpu as pltpu

ATTN_Q_HEADS = 8
ATTN_KV_HEADS = 2
ATTN_HEAD_DIM = 64
WINDOW = 128
REL_BUCKETS = 32
REL_MAX_DIST = 128
DN_HEADS = 4
DN_HEAD_DIM = 128
DN_CONV = 4
DN_CHUNK = 128
D_FF = 2816
FFN_CONV = 3
RMS_EPS = 1e-6
L2_EPS = 1e-6
N_MOD = 6
NEG_INF = -1e30

ATTN_Q_DIM = ATTN_Q_HEADS * ATTN_HEAD_DIM
ATTN_KV_DIM = ATTN_KV_HEADS * ATTN_HEAD_DIM
DN_DIM = DN_HEADS * DN_HEAD_DIM
ATTN_GROUP = ATTN_Q_HEADS // ATTN_KV_HEADS

V7X_LANES = 128
V7X_SUBLANES = 8
V7X_VMEM_BYTES = 64 * 1024 * 1024

F32 = jnp.float32
BF16 = jnp.bfloat16

ROW_TILE = 1024
IN_ROW_TILE = 512
DN_ACT_COLS = 256
SPLIT_ROWS = 2
ATTN_BLOCKS = 4
DN_ROWS = 256
DN_BATCH = 2
FF_CHUNK = 256
BD_PAD = V7X_LANES


def _vmem_limit(nbytes):
    return int(min(V7X_VMEM_BYTES, nbytes + nbytes // 2 + (8 << 20)))


def _resident(shape):
    nd = len(shape)
    return pl.BlockSpec(shape, lambda *_: (0,) * nd, pipeline_mode=pl.Buffered(1))


def _dot(a, b):
    return jnp.dot(a, b, preferred_element_type=F32)


def _dot_nt(a, b):
    return lax.dot_general(a, b, (((1,), (1,)), ((), ())), preferred_element_type=F32)


def _dot_tn(a, b):
    return lax.dot_general(a, b, (((0,), (0,)), ((), ())), preferred_element_type=F32)


def _rms(xf, w):
    return xf * lax.rsqrt(jnp.mean(xf * xf, axis=-1, keepdims=True) + RMS_EPS) * w


def _sigmoid(x):
    return 1.0 / (1.0 + jnp.exp(-x))


def _silu(x):
    return x * _sigmoid(x)


def _mod_body(c_ref, w_ref, b_ref, o_ref):
    ca = _silu(c_ref[...])
    o_ref[...] = _dot(ca.astype(BF16), w_ref[...].astype(BF16)) + b_ref[...]


def _modulation(c, ada_w, ada_b):
    bsz, d = c.shape
    n = ada_w.shape[1]
    tn = d
    return pl.pallas_call(
        _mod_body,
        grid=(n // tn,),
        in_specs=[
            pl.BlockSpec((bsz, d), lambda j: (0, 0)),
            pl.BlockSpec((d, tn), lambda j: (0, j)),
            pl.BlockSpec((1, tn), lambda j: (0, j)),
        ],
        out_specs=pl.BlockSpec((bsz, tn), lambda j: (0, j)),
        out_shape=jax.ShapeDtypeStruct((bsz, n), F32),
        compiler_params=pltpu.CompilerParams(
            dimension_semantics=("arbitrary",), vmem_limit_bytes=_vmem_limit(2 * d * tn * 4)),
        name="adaln_modulation",
    )(c, ada_w, ada_b.reshape(1, n))


def _t5_bucket_band():
    qi = np.arange(WINDOW)[:, None]
    kj = np.arange(2 * WINDOW)[None, :]
    dist = np.maximum(WINDOW + qi - kj, 0)
    max_exact = REL_BUCKETS // 2
    scaled = np.log(np.maximum(dist, 1).astype(np.float32) / np.float32(max_exact)) / np.float32(
        math.log(REL_MAX_DIST / max_exact))
    large = max_exact + (scaled.astype(np.float32) * np.float32(REL_BUCKETS - max_exact)).astype(np.int32)
    large = np.minimum(large, REL_BUCKETS - 1)
    return np.where(dist < max_exact, dist, large).astype(np.int32)


def _bias_body(rb_ref, bucket_ref, o_ref):
    h = pl.program_id(0)
    bucket = bucket_ref[...]
    acc = jnp.zeros(bucket.shape, F32)
    for v in range(REL_BUCKETS):
        acc = jnp.where(bucket == v, rb_ref[v, h], acc)
    o_ref[0] = acc


def _rel_bias_table(rel_bias):
    bucket = jnp.asarray(_t5_bucket_band())
    return pl.pallas_call(
        _bias_body,
        grid=(ATTN_Q_HEADS,),
        in_specs=[
            pl.BlockSpec(memory_space=pltpu.SMEM),
            pl.BlockSpec((WINDOW, 2 * WINDOW), lambda h: (0, 0)),
        ],
        out_specs=pl.BlockSpec((1, WINDOW, 2 * WINDOW), lambda h: (h, 0, 0)),
        out_shape=jax.ShapeDtypeStruct((ATTN_Q_HEADS, WINDOW, 2 * WINDOW), F32),
        compiler_params=pltpu.CompilerParams(dimension_semantics=("arbitrary",)),
        name="rel_bias_table",
    )(rel_bias.astype(F32), bucket)


_IN_GROUPS = (
    ("q", ATTN_Q_DIM, BF16),
    ("k", ATTN_KV_DIM, BF16),
    ("v", ATTN_KV_DIM, BF16),
    ("dqkv", 3 * DN_DIM, BF16),
    ("dz", DN_DIM, BF16),
    ("ga", None, BF16),
    ("gd", None, BF16),
    ("bd", BD_PAD, F32),
)


def _in_groups(d_model):
    return tuple((n, d_model if w is None else w, dt) for n, w, dt in _IN_GROUPS)


def _pack_w_in(w_in):
    d = w_in.shape[0]
    o = 0
    aq = w_in[:, o:o + ATTN_Q_DIM] * (ATTN_HEAD_DIM ** -0.5); o += ATTN_Q_DIM
    ak = w_in[:, o:o + ATTN_KV_DIM]; o += ATTN_KV_DIM
    av = w_in[:, o:o + ATTN_KV_DIM]; o += ATTN_KV_DIM
    dqkv = w_in[:, o:o + 3 * DN_DIM]; o += 3 * DN_DIM
    dz = w_in[:, o:o + DN_DIM]; o += DN_DIM
    bd = w_in[:, o:o + 2 * DN_HEADS]; o += 2 * DN_HEADS
    ga = w_in[:, o:o + d]; o += d
    gd = w_in[:, o:o + d]; o += d
    bd = jnp.pad(bd, ((0, 0), (0, BD_PAD - 2 * DN_HEADS)))
    return jnp.concatenate([aq, ak, av, dqkv, dz, ga, gd, bd], axis=1).astype(BF16)


def _dn_activate(y, lo, cw_ref, xpad_ref, o_ref):
    rows, width = y.shape
    sub = V7X_SUBLANES
    cols = slice(lo, lo + width)
    xpad_ref[sub:sub + rows, cols] = y
    acc = cw_ref[DN_CONV - 1:DN_CONV, cols] * y
    for j in range(DN_CONV - 1):
        start = sub - (DN_CONV - 1) + j
        acc = acc + cw_ref[j:j + 1, cols] * xpad_ref[start:start + rows, cols]
    xpad_ref[0:sub, cols] = y[rows - sub:rows, :]
    a = _silu(acc)
    if lo < 2 * DN_DIM:
        heads = []
        for h in range(width // DN_HEAD_DIM):
            t = a[:, h * DN_HEAD_DIM:(h + 1) * DN_HEAD_DIM]
            t = t * lax.rsqrt(jnp.sum(t * t, axis=-1, keepdims=True) + L2_EPS)
            heads.append(t * (DN_HEAD_DIM ** -0.5) if lo < DN_DIM else t)
        a = jnp.concatenate(heads, axis=1)
    o_ref[0, :, cols] = a.astype(o_ref.dtype)


def _inproj_body(groups, x_ref, sc_ref, sh_ref, nw_ref, w_ref, cw_ref, *refs):
    out_refs, xpad_ref = refs[:-1], refs[-1]

    @pl.when(pl.program_id(1) == 0)
    def _():
        xpad_ref[0:V7X_SUBLANES, :] = jnp.zeros((V7X_SUBLANES, xpad_ref.shape[1]), F32)

    xf = x_ref[0]
    u = (_rms(xf, nw_ref[...]) * (1.0 + sc_ref[0]) + sh_ref[0]).astype(BF16)
    offs = np.cumsum([0] + [w for _, w, _ in groups])
    order = sorted(range(len(groups)), key=lambda i: groups[i][0] != "dqkv")
    for i in order:
        (name, width, dt), o_ref, off = groups[i], out_refs[i], int(offs[i])
        if name == "dqkv":
            for lo in range(0, width, DN_ACT_COLS):
                _dn_activate(_dot(u, w_ref[:, off + lo:off + lo + DN_ACT_COLS]), lo, cw_ref, xpad_ref, o_ref)
        else:
            o_ref[0] = _dot(u, w_ref[:, off:off + width]).astype(dt)


def _in_projection(x, sc1, sh1, norm_w, w_packed, dn_conv_w):
    bsz, seq, d = x.shape
    groups = _in_groups(d)
    n_tot = sum(w for _, w, _ in groups)
    tm = min(IN_ROW_TILE, seq)
    row = lambda width: pl.BlockSpec((1, tm, width), lambda b, t: (b, t, 0))
    per_batch = pl.BlockSpec((1, 1, d), lambda b, t: (b, 0, 0))
    out_bytes = sum(2 * tm * w * jnp.dtype(dt).itemsize for _, w, dt in groups)
    est = 2 * tm * d * 4 + d * n_tot * 2 + out_bytes + 4 * tm * d * 4 + 6 * tm * 3 * DN_DIM * 4
    return pl.pallas_call(
        functools.partial(_inproj_body, groups),
        grid=(bsz, seq // tm),
        in_specs=[row(d), per_batch, per_batch, _resident((1, d)), _resident((d, n_tot)),
                  _resident((DN_CONV, 3 * DN_DIM))],
        out_specs=[row(w) for _, w, _ in groups],
        out_shape=[jax.ShapeDtypeStruct((bsz, seq, w), dt) for _, w, dt in groups],
        scratch_shapes=[pltpu.VMEM((tm + V7X_SUBLANES, 3 * DN_DIM), F32)],
        compiler_params=pltpu.CompilerParams(
            dimension_semantics=("arbitrary", "arbitrary"), vmem_limit_bytes=_vmem_limit(est)),
        name="in_projection",
    )(x, sc1, sh1, norm_w.reshape(1, d), w_packed, dn_conv_w.astype(F32))


def _attn_body(sink_ref, q_ref, kp_ref, kc_ref, vp_ref, vc_ref, bias_ref, o_ref):
    n = pl.program_id(1)
    n_q = q_ref.shape[1] // WINDOW
    k_all = jnp.concatenate([kp_ref[0], kc_ref[0]], axis=0)
    v_all = jnp.concatenate([vp_ref[0], vc_ref[0]], axis=0)
    qi = lax.broadcasted_iota(jnp.int32, (WINDOW, 2 * WINDOW), 0)
    kj = lax.broadcasted_iota(jnp.int32, (WINDOW, 2 * WINDOW), 1)
    dist = WINDOW + qi - kj
    in_band = (dist >= 0) & (dist < WINDOW)
    first_valid = in_band & ((kj >= WINDOW) | (n > 0))
    units = [(j, h) for j in range(n_q) for h in range(ATTN_Q_HEADS)]
    head_cols = lambda x, i: x[:, i * ATTN_HEAD_DIM:(i + 1) * ATTN_HEAD_DIM]
    band = lambda x, j: x[j * WINDOW:(j + 2) * WINDOW]
    scores = [jnp.where(first_valid if j == 0 else in_band,
                        _dot_nt(head_cols(q_ref[0, j * WINDOW:(j + 1) * WINDOW, :], h),
                                head_cols(band(k_all, j), h // ATTN_GROUP)) + bias_ref[h], NEG_INF)
              for j, h in units]
    maxes = [jnp.maximum(jnp.max(s, axis=-1, keepdims=True), sink_ref[h]) for s, (_, h) in zip(scores, units)]
    probs = [jnp.exp(s - m) for s, m in zip(scores, maxes)]
    denoms = [jnp.sum(p, axis=-1, keepdims=True) + jnp.exp(sink_ref[h] - m)
              for p, m, (_, h) in zip(probs, maxes, units)]
    outs = [_dot(p.astype(BF16), head_cols(band(v_all, j), h // ATTN_GROUP)) / d
            for p, d, (j, h) in zip(probs, denoms, units)]
    for j in range(n_q):
        o_ref[0, j * WINDOW:(j + 1) * WINDOW, :] = jnp.concatenate(
            outs[j * ATTN_Q_HEADS:(j + 1) * ATTN_Q_HEADS], axis=1).astype(o_ref.dtype)


def _attention(q, k, v, sinks, bias):
    bsz, seq, _ = q.shape
    n_q = math.gcd(ATTN_BLOCKS, seq // WINDOW)
    cur = lambda width: pl.BlockSpec((1, n_q * WINDOW, width), lambda b, n: (b, n, 0))
    prev = lambda width: pl.BlockSpec((1, WINDOW, width), lambda b, n: (b, jnp.maximum(n * n_q - 1, 0), 0))
    return pl.pallas_call(
        _attn_body,
        grid=(bsz, seq // (n_q * WINDOW)),
        in_specs=[
            pl.BlockSpec(memory_space=pltpu.SMEM),
            cur(ATTN_Q_DIM), prev(ATTN_KV_DIM), cur(ATTN_KV_DIM), prev(ATTN_KV_DIM), cur(ATTN_KV_DIM),
            _resident((ATTN_Q_HEADS, WINDOW, 2 * WINDOW)),
        ],
        out_specs=cur(ATTN_Q_DIM),
        out_shape=jax.ShapeDtypeStruct((bsz, seq, ATTN_Q_DIM), BF16),
        compiler_params=pltpu.CompilerParams(dimension_semantics=("arbitrary", "arbitrary")),
        name="swa_attention",
    )(sinks.astype(F32), q, k, k, v, v, bias)


def _split3(a):
    hi = a.astype(BF16)
    r = a - hi.astype(F32)
    mid = r.astype(BF16)
    lo = (r - mid.astype(F32)).astype(BF16)
    return hi, mid, lo


def _mm(a, b, dot=_dot):
    return dot(a.astype(BF16), b.astype(BF16))


def _unit_lower_inverses(lowers, ii, jj):
    size = lowers[0].shape[0]
    eye = jnp.where(ii == jj, 1.0, 0.0)
    invs = [eye] * len(lowers)
    s = 1
    while s < size:
        sh = int(math.log2(s))
        joins = (jnp.right_shift(ii, sh + 1) == jnp.right_shift(jj, sh + 1)) & (
            jnp.right_shift(ii, sh) != jnp.right_shift(jj, sh))
        es = [jnp.where(joins, lower, 0.0) for lower in lowers]
        if s == 1:
            invs = [inv - e for inv, e in zip(invs, es)]
        else:
            tmps = [_mm(e, inv) for e, inv in zip(es, invs)]
            invs = [inv - _mm(inv, tmp) for inv, tmp in zip(invs, tmps)]
        s *= 2
    return invs


def _dn_body(qkv_ref, z_ref, bd_ref, alog_ref, dtb_ref, nw_ref, o_ref, state_ref):
    n_b, rows = qkv_ref.shape[0], qkv_ref.shape[1]
    hd = DN_HEAD_DIM

    @pl.when(pl.program_id(1) == 0)
    def _():
        state_ref[...] = jnp.zeros_like(state_ref)

    cc = DN_CHUNK
    n_c = rows // cc
    ri = lax.broadcasted_iota(jnp.int32, (rows, rows), 0)
    ci = lax.broadcasted_iota(jnp.int32, (rows, rows), 1)
    same_chunk = jnp.right_shift(ri, int(math.log2(cc))) == jnp.right_shift(ci, int(math.log2(cc)))
    tri = jnp.where(same_chunk & (ci <= ri), 1.0, 0.0).astype(BF16)
    ii = lax.broadcasted_iota(jnp.int32, (cc, cc), 0)
    jj = lax.broadcasted_iota(jnp.int32, (cc, cc), 1)
    incl = jj <= ii
    strict = jj < ii
    nw = nw_ref[...]

    q_l, k_l, rhs_l, gcc_l, eg_l, gend_l, lower_l, intra_l = [], [], [], [], [], [], [], []
    for b in range(n_b):
        bd = bd_ref[b]
        beta_all = _sigmoid(bd)
        g_all = -jnp.exp(alog_ref[...]) * jax.nn.softplus(bd + dtb_ref[...])
        g_hi, g_mid, g_lo = _split3(g_all)
        gc = _dot(tri, g_hi) + _dot(tri, g_mid) + _dot(tri, g_lo)
        gc_t = gc.T
        eg_all = jnp.exp(gc)
        for h in range(DN_HEADS):
            q_all = qkv_ref[b, :, h * hd:(h + 1) * hd].astype(F32)
            k_all = qkv_ref[b, :, DN_DIM + h * hd:DN_DIM + (h + 1) * hd].astype(F32)
            v_all = qkv_ref[b, :, 2 * DN_DIM + h * hd:2 * DN_DIM + (h + 1) * hd].astype(F32)
            col = DN_HEADS + h
            for c in range(n_c):
                r0 = c * cc
                q = q_all[r0:r0 + cc]
                k = k_all[r0:r0 + cc]
                beta = beta_all[r0:r0 + cc, h:h + 1]
                gcc = gc[r0:r0 + cc, col:col + 1]
                gcr = gc_t[col:col + 1, r0:r0 + cc]
                eg = eg_all[r0:r0 + cc, col:col + 1]
                decay = jnp.where(incl, jnp.exp(jnp.where(incl, gcc - gcr, 0.0)), 0.0)
                kb = k * beta
                q_l.append(q); k_l.append(k); gcc_l.append(gcc); eg_l.append(eg)
                gend_l.append(gc[r0 + cc - 1:r0 + cc, col:col + 1])
                rhs_l.append(jnp.concatenate([v_all[r0:r0 + cc] * beta, kb * eg], axis=1))
                lower_l.append(jnp.where(strict, _mm(kb, k, _dot_nt) * decay, 0.0))
                intra_l.append(jnp.where(incl, _mm(q, k, _dot_nt) * decay, 0.0))
    inv_l = _unit_lower_inverses(lower_l, ii, jj)
    sol_l = [_mm(inv, rhs) for inv, rhs in zip(inv_l, rhs_l)]

    pairs = [(b, h) for b in range(n_b) for h in range(DN_HEADS)]
    unit = lambda b, h, c: (b * DN_HEADS + h) * n_c + c
    state_l = [state_ref[b, h] for b, h in pairs]
    for c in range(n_c):
        r0 = c * cc
        us = [unit(b, h, c) for b, h in pairs]
        vnew_l = [sol_l[u][:, :hd] - _mm(sol_l[u][:, hd:], state) for u, state in zip(us, state_l)]
        o_l = [_mm(q_l[u] * eg_l[u], state) + _mm(intra_l[u], vnew)
               for u, state, vnew in zip(us, state_l, vnew_l)]
        state_l = [state * jnp.exp(gend_l[u]) + _mm(k_l[u] * jnp.exp(gend_l[u] - gcc_l[u]), vnew, _dot_tn)
                   for u, state, vnew in zip(us, state_l, vnew_l)]
        for (b, h), o in zip(pairs, o_l):
            y = _rms(o, nw) * _silu(z_ref[b, r0:r0 + cc, h * hd:(h + 1) * hd].astype(F32))
            o_ref[b, r0:r0 + cc, h * hd:(h + 1) * hd] = y.astype(o_ref.dtype)
    for (b, h), state in zip(pairs, state_l):
        state_ref[b, h] = state


def _deltanet(qkv_act, dz, bd, a_log, dt_bias, norm_w):
    bsz, seq, _ = qkv_act.shape
    rows = min(DN_ROWS, seq)
    n_b = math.gcd(DN_BATCH, bsz)
    lane_row = lambda vec: jnp.pad(vec.astype(F32), (DN_HEADS, BD_PAD - 2 * DN_HEADS)).reshape(1, BD_PAD)
    blk = lambda width: pl.BlockSpec((n_b, rows, width), lambda b, s: (b, s, 0))
    est = n_b * (2 * rows * (3 * DN_DIM * 2 + DN_DIM * 2 + BD_PAD * 4 + DN_DIM * 2) + 6 * rows * 3 * DN_DIM * 4
                 + DN_HEADS * 24 * rows * DN_CHUNK * 4)
    return pl.pallas_call(
        _dn_body,
        grid=(bsz // n_b, seq // rows),
        in_specs=[
            blk(3 * DN_DIM), blk(DN_DIM), blk(BD_PAD),
            _resident((1, BD_PAD)), _resident((1, BD_PAD)), _resident((1, DN_HEAD_DIM)),
        ],
        out_specs=blk(DN_DIM),
        out_shape=jax.ShapeDtypeStruct((bsz, seq, DN_DIM), BF16),
        scratch_shapes=[pltpu.VMEM((n_b, DN_HEADS, DN_HEAD_DIM, DN_HEAD_DIM), F32)],
        compiler_params=pltpu.CompilerParams(
            dimension_semantics=("arbitrary", "arbitrary"), vmem_limit_bytes=_vmem_limit(est)),
        name="gated_deltanet",
    )(qkv_act, dz, bd, lane_row(a_log), lane_row(dt_bias), norm_w.astype(F32).reshape(1, -1))


def _mix_body(x_ref, ya_ref, yd_ref, ga_ref, gd_ref, wab_ref, wdb_ref, wo_ref, npost_ref, g1_ref, npre_ref,
              sc2_ref, sh2_ref, h_ref, u_ref):
    tm = x_ref.shape[1]
    half = tm // SPLIT_ROWS
    groups = [slice(r0, r0 + half) for r0 in range(0, tm, half)]
    branch = [(_dot(ya_ref[0, rs, :], wab_ref[...]), _dot(yd_ref[0, rs, :], wdb_ref[...])) for rs in groups]
    ys = []
    for rs, (ba, bdn) in zip(groups, branch):
        merged = (_sigmoid(ga_ref[0, rs, :].astype(F32)) * ba
                  + _sigmoid(gd_ref[0, rs, :].astype(F32)) * bdn)
        ys.append(_dot(merged.astype(BF16), wo_ref[...]))
    for rs, y in zip(groups, ys):
        h1 = x_ref[0, rs, :] + g1_ref[0] * _rms(y, npost_ref[...])
        h_ref[0, rs, :] = h1
        u_ref[0, rs, :] = (_rms(h1, npre_ref[...]) * (1.0 + sc2_ref[0]) + sh2_ref[0]).astype(u_ref.dtype)


def _mix_out(x, ya, yd, ga, gd, w_ab, w_db, w_out, norm_post, g1, norm_pre, sc2, sh2):
    bsz, seq, d = x.shape
    tm = min(ROW_TILE, seq)
    row = lambda width: pl.BlockSpec((1, tm, width), lambda b, t: (b, t, 0))
    per_batch = pl.BlockSpec((1, 1, d), lambda b, t: (b, 0, 0))
    est = (2 * tm * d * (4 + 2 + 2 + 4 + 2) + 2 * tm * 2 * ATTN_Q_DIM * 2 + (2 * ATTN_Q_DIM + d) * d * 2
           + 6 * tm * d * 4)
    return pl.pallas_call(
        _mix_body,
        grid=(bsz, seq // tm),
        in_specs=[
            row(d), row(ATTN_Q_DIM), row(DN_DIM), row(d), row(d),
            _resident((ATTN_Q_DIM, d)), _resident((DN_DIM, d)), _resident((d, d)),
            _resident((1, d)), per_batch, _resident((1, d)), per_batch, per_batch,
        ],
        out_specs=[row(d), row(d)],
        out_shape=[jax.ShapeDtypeStruct((bsz, seq, d), F32), jax.ShapeDtypeStruct((bsz, seq, d), BF16)],
        compiler_params=pltpu.CompilerParams(
            dimension_semantics=("arbitrary", "arbitrary"), vmem_limit_bytes=_vmem_limit(est)),
        name="merge_out_projection",
    )(x, ya, yd, ga, gd, w_ab, w_db, w_out, norm_post.reshape(1, d), g1, norm_pre.reshape(1, d), sc2, sh2)


def _ffn_body(u_ref, h_ref, wup_ref, cw_ref, wdn_ref, npost_ref, g2_ref, o_ref, gpad_ref, vpad_ref, gcar_ref,
              vcar_ref, act_ref):
    tm = u_ref.shape[1]
    n_ch = D_FF // FF_CHUNK
    sub = V7X_SUBLANES

    @pl.when(pl.program_id(1) == 0)
    def _():
        gcar_ref[...] = jnp.zeros_like(gcar_ref)
        vcar_ref[...] = jnp.zeros_like(vcar_ref)

    u = u_ref[0]

    def conv_half(pad_ref, car_ref, j, off):
        up = _dot(u, wup_ref[:, off:off + FF_CHUNK])
        pad_ref[0:sub, :] = car_ref[j]
        pad_ref[sub:sub + tm, :] = up
        car_ref[j] = up[tm - sub:tm, :]
        out = cw_ref[FFN_CONV - 1:FFN_CONV, off:off + FF_CHUNK] * up
        for t in range(FFN_CONV - 1):
            start = sub - (FFN_CONV - 1) + t
            out = out + cw_ref[t:t + 1, off:off + FF_CHUNK] * pad_ref[start:start + tm, :]
        return out

    for j in range(n_ch):
        gate = conv_half(gpad_ref, gcar_ref, j, j * FF_CHUNK)
        val = conv_half(vpad_ref, vcar_ref, j, D_FF + j * FF_CHUNK)
        act_ref[:, j * FF_CHUNK:(j + 1) * FF_CHUNK] = (jax.nn.gelu(gate, approximate=True) * val).astype(BF16)

    y = _dot(act_ref[...], wdn_ref[...])
    o_ref[0] = h_ref[0] + g2_ref[0] * _rms(y, npost_ref[...])


def _ffn(u2, h1, w_up, conv_w, w_down, norm_post, g2):
    bsz, seq, d = h1.shape
    tm = min(ROW_TILE, seq)
    n_ch = D_FF // FF_CHUNK
    row = lambda width: pl.BlockSpec((1, tm, width), lambda b, t: (b, t, 0))
    per_batch = pl.BlockSpec((1, 1, d), lambda b, t: (b, 0, 0))
    est = (2 * tm * d * (2 + 4 + 4) + 3 * d * D_FF * 2 + 2 * (tm + 8) * FF_CHUNK * 4 + tm * D_FF * 2
           + 8 * tm * FF_CHUNK * 4 + 2 * tm * d * 4)
    return pl.pallas_call(
        _ffn_body,
        grid=(bsz, seq // tm),
        in_specs=[
            row(d), row(d),
            _resident((d, 2 * D_FF)), _resident((FFN_CONV, 2 * D_FF)), _resident((D_FF, d)),
            _resident((1, d)), per_batch,
        ],
        out_specs=row(d),
        out_shape=jax.ShapeDtypeStruct((bsz, seq, d), F32),
        scratch_shapes=[
            pltpu.VMEM((tm + V7X_SUBLANES, FF_CHUNK), F32),
            pltpu.VMEM((tm + V7X_SUBLANES, FF_CHUNK), F32),
            pltpu.VMEM((n_ch, V7X_SUBLANES, FF_CHUNK), F32),
            pltpu.VMEM((n_ch, V7X_SUBLANES, FF_CHUNK), F32),
            pltpu.VMEM((tm, D_FF), BF16),
        ],
        compiler_params=pltpu.CompilerParams(
            dimension_semantics=("arbitrary", "arbitrary"), vmem_limit_bytes=_vmem_limit(est)),
        name="conv_geglu_ffn",
    )(u2, h1, w_up, conv_w.astype(F32), w_down, norm_post.reshape(1, d), g2)


def _layer(h, c, ada_w, ada_b, norm_mix_pre, norm_mix_post, norm_ffn_pre, norm_ffn_post, w_in, dn_conv_w,
           dn_a_log, dn_dt_bias, dn_norm_w, attn_sinks, bias, w_attn_branch, w_dn_branch, w_out, ffn_w_up,
           ffn_conv_w, ffn_w_down):
    bsz, seq, d = h.shape
    mod = _modulation(c, ada_w, ada_b)
    sh1, sc1, g1, sh2, sc2, g2 = [mod[:, i * d:(i + 1) * d].reshape(bsz, 1, d) for i in range(N_MOD)]
    q, k, v, dqkv, dz, ga, gd, bd = _in_projection(h, sc1, sh1, norm_mix_pre, _pack_w_in(w_in), dn_conv_w)
    y_attn = _attention(q, k, v, attn_sinks, bias)
    y_dn = _deltanet(dqkv, dz, bd, dn_a_log, dn_dt_bias, dn_norm_w)
    h1, u2 = _mix_out(h, y_attn, y_dn, ga, gd, w_attn_branch.astype(BF16), w_dn_branch.astype(BF16),
                      w_out.astype(BF16), norm_mix_post, g1, norm_ffn_pre, sc2, sh2)
    return _ffn(u2, h1, ffn_w_up.astype(BF16), ffn_conv_w, ffn_w_down.astype(BF16), norm_ffn_post, g2)


def kernel(x, c, ada_w, ada_b, norm_mix_pre, norm_mix_post, norm_ffn_pre, norm_ffn_post, w_in, dn_conv_w, dn_a_log, dn_dt_bias, dn_norm_w, attn_sinks, rel_bias, w_attn_branch, w_dn_branch, w_out, ffn_w_up, ffn_conv_w, ffn_w_down):
    bias = _rel_bias_table(rel_bias)
    h = x
    for l in range(ada_w.shape[0]):
        h = _layer(h, c, ada_w[l], ada_b[l], norm_mix_pre[l], norm_mix_post[l], norm_ffn_pre[l],
                   norm_ffn_post[l], w_in[l], dn_conv_w[l], dn_a_log[l], dn_dt_bias[l], dn_norm_w[l],
                   attn_sinks[l], bias, w_attn_branch[l], w_dn_branch[l], w_out[l], ffn_w_up[l],
                   ffn_conv_w[l], ffn_w_down[l])
    return h
```

```python
import functools
import math

import numpy as np
import jax
import jax.numpy as jnp
from jax import lax
from jax.experimental import pallas as pl
from jax.experimental.pallas import tpu as pltpu

ATTN_Q_HEADS = 8
ATTN_KV_HEADS = 2
ATTN_HEAD_DIM = 64
WINDOW = 128
REL_BUCKETS = 32
REL_MAX_DIST = 128
DN_HEADS = 4
DN_HEAD_DIM = 128
DN_CONV = 4
DN_CHUNK = 128
D_FF = 2816
FFN_CONV = 3
RMS_EPS = 1e-6
L2_EPS = 1e-6
N_MOD = 6
NEG_INF = -1e30

ATTN_Q_DIM = ATTN_Q_HEADS * ATTN_HEAD_DIM
ATTN_KV_DIM = ATTN_KV_HEADS * ATTN_HEAD_DIM
DN_DIM = DN_HEADS * DN_HEAD_DIM
ATTN_GROUP = ATTN_Q_HEADS // ATTN_KV_HEADS

V7X_LANES = 128
V7X_SUBLANES = 8
V7X_VMEM_BYTES = 64 * 1024 * 1024

F32 = jnp.float32
BF16 = jnp.bfloat16

ROW_TILE = 1024
IN_ROW_TILE = 512
DN_ACT_COLS = 256
SPLIT_ROWS = 4
ATTN_BLOCKS = 8
DN_ROWS = 256
DN_BATCH = 2
FF_CHUNK = 256
BD_PAD = V7X_LANES


def _vmem_limit(nbytes):
    return int(min(V7X_VMEM_BYTES, nbytes + nbytes // 2 + (8 << 20)))


def _resident(shape):
    nd = len(shape)
    return pl.BlockSpec(shape, lambda *_: (0,) * nd, pipeline_mode=pl.Buffered(1))


def _dot(a, b):
    return jnp.dot(a, b, preferred_element_type=F32)


def _dot_nt(a, b):
    return lax.dot_general(a, b, (((1,), (1,)), ((), ())), preferred_element_type=F32)


def _dot_tn(a, b):
    return lax.dot_general(a, b, (((0,), (0,)), ((), ())), preferred_element_type=F32)


def _rms(xf, w):
    return xf * lax.rsqrt(jnp.mean(xf * xf, axis=-1, keepdims=True) + RMS_EPS) * w


def _sigmoid(x):
    return 1.0 / (1.0 + jnp.exp(-x))


def _silu(x):
    return x * _sigmoid(x)


def _mod_body(c_ref, w_ref, b_ref, o_ref):
    ca = _silu(c_ref[...])
    o_ref[...] = _dot(ca.astype(BF16), w_ref[...].astype(BF16)) + b_ref[...]


def _modulation(c, ada_w, ada_b):
    bsz, d = c.shape
    n = ada_w.shape[1]
    tn = d
    return pl.pallas_call(
        _mod_body,
        grid=(n // tn,),
        in_specs=[
            pl.BlockSpec((bsz, d), lambda j: (0, 0)),
            pl.BlockSpec((d, tn), lambda j: (0, j)),
            pl.BlockSpec((1, tn), lambda j: (0, j)),
        ],
        out_specs=pl.BlockSpec((bsz, tn), lambda j: (0, j)),
        out_shape=jax.ShapeDtypeStruct((bsz, n), F32),
        compiler_params=pltpu.CompilerParams(
            dimension_semantics=("arbitrary",), vmem_limit_bytes=_vmem_limit(2 * d * tn * 4)),
        name="adaln_modulation",
    )(c, ada_w, ada_b.reshape(1, n))


def _t5_bucket_band():
    qi = np.arange(WINDOW)[:, None]
    kj = np.arange(2 * WINDOW)[None, :]
    dist = np.maximum(WINDOW + qi - kj, 0)
    max_exact = REL_BUCKETS // 2
    scaled = np.log(np.maximum(dist, 1).astype(np.float32) / np.float32(max_exact)) / np.float32(
        math.log(REL_MAX_DIST / max_exact))
    large = max_exact + (scaled.astype(np.float32) * np.float32(REL_BUCKETS - max_exact)).astype(np.int32)
    large = np.minimum(large, REL_BUCKETS - 1)
    return np.where(dist < max_exact, dist, large).astype(np.int32)


def _bias_body(rb_ref, bucket_ref, o_ref):
    h = pl.program_id(0)
    bucket = bucket_ref[...]
    acc = jnp.zeros(bucket.shape, F32)
    for v in range(REL_BUCKETS):
        acc = jnp.where(bucket == v, rb_ref[v, h], acc)
    o_ref[0] = acc


def _rel_bias_table(rel_bias):
    bucket = jnp.asarray(_t5_bucket_band())
    return pl.pallas_call(
        _bias_body,
        grid=(ATTN_Q_HEADS,),
        in_specs=[
            pl.BlockSpec(memory_space=pltpu.SMEM),
            pl.BlockSpec((WINDOW, 2 * WINDOW), lambda h: (0, 0)),
        ],
        out_specs=pl.BlockSpec((1, WINDOW, 2 * WINDOW), lambda h: (h, 0, 0)),
        out_shape=jax.ShapeDtypeStruct((ATTN_Q_HEADS, WINDOW, 2 * WINDOW), F32),
        compiler_params=pltpu.CompilerParams(dimension_semantics=("arbitrary",)),
        name="rel_bias_table",
    )(rel_bias.astype(F32), bucket)


_IN_GROUPS = (
    ("q", ATTN_Q_DIM, BF16),
    ("k", ATTN_KV_DIM, BF16),
    ("v", ATTN_KV_DIM, BF16),
    ("dqkv", 3 * DN_DIM, BF16),
    ("dz", DN_DIM, BF16),
    ("ga", None, BF16),
    ("gd", None, BF16),
    ("bd", BD_PAD, F32),
)


def _in_groups(d_model):
    return tuple((n, d_model if w is None else w, dt) for n, w, dt in _IN_GROUPS)


def _pack_w_in(w_in):
    d = w_in.shape[0]
    o = 0
    aq = w_in[:, o:o + ATTN_Q_DIM] * (ATTN_HEAD_DIM ** -0.5); o += ATTN_Q_DIM
    ak = w_in[:, o:o + ATTN_KV_DIM]; o += ATTN_KV_DIM
    av = w_in[:, o:o + ATTN_KV_DIM]; o += ATTN_KV_DIM
    dqkv = w_in[:, o:o + 3 * DN_DIM]; o += 3 * DN_DIM
    dz = w_in[:, o:o + DN_DIM]; o += DN_DIM
    bd = w_in[:, o:o + 2 * DN_HEADS]; o += 2 * DN_HEADS
    ga = w_in[:, o:o + d]; o += d
    gd = w_in[:, o:o + d]; o += d
    bd = jnp.pad(bd, ((0, 0), (0, BD_PAD - 2 * DN_HEADS)))
    return jnp.concatenate([aq, ak, av, dqkv, dz, ga, gd, bd], axis=1).astype(BF16)


def _dn_activate(y, lo, cw_ref, xpad_ref, o_ref):
    rows, width = y.shape
    sub = V7X_SUBLANES
    cols = slice(lo, lo + width)
    xpad_ref[sub:sub + rows, cols] = y
    acc = cw_ref[DN_CONV - 1:DN_CONV, cols] * y
    for j in range(DN_CONV - 1):
        start = sub - (DN_CONV - 1) + j
        acc = acc + cw_ref[j:j + 1, cols] * xpad_ref[start:start + rows, cols]
    xpad_ref[0:sub, cols] = y[rows - sub:rows, :]
    a = _silu(acc)
    if lo < 2 * DN_DIM:
        heads = []
        for h in range(width // DN_HEAD_DIM):
            t = a[:, h * DN_HEAD_DIM:(h + 1) * DN_HEAD_DIM]
            t = t * lax.rsqrt(jnp.sum(t * t, axis=-1, keepdims=True) + L2_EPS)
            heads.append(t * (DN_HEAD_DIM ** -0.5) if lo < DN_DIM else t)
        a = jnp.concatenate(heads, axis=1)
    o_ref[0, :, cols] = a.astype(o_ref.dtype)


def _inproj_body(groups, x_ref, sc_ref, sh_ref, nw_ref, w_ref, cw_ref, *refs):
    out_refs, xpad_ref = refs[:-1], refs[-1]

    @pl.when(pl.program_id(1) == 0)
    def _():
        xpad_ref[0:V7X_SUBLANES, :] = jnp.zeros((V7X_SUBLANES, xpad_ref.shape[1]), F32)

    xf = x_ref[0]
    u = (_rms(xf, nw_ref[...]) * (1.0 + sc_ref[0]) + sh_ref[0]).astype(BF16)
    offs = np.cumsum([0] + [w for _, w, _ in groups])
    order = sorted(range(len(groups)), key=lambda i: groups[i][0] != "dqkv")
    for i in order:
        (name, width, dt), o_ref, off = groups[i], out_refs[i], int(offs[i])
        if name == "dqkv":
            for lo in range(0, width, DN_ACT_COLS):
                _dn_activate(_dot(u, w_ref[:, off + lo:off + lo + DN_ACT_COLS]), lo, cw_ref, xpad_ref, o_ref)
        else:
            o_ref[0] = _dot(u, w_ref[:, off:off + width]).astype(dt)


def _in_projection(x, sc1, sh1, norm_w, w_packed, dn_conv_w):
    bsz, seq, d = x.shape
    groups = _in_groups(d)
    n_tot = sum(w for _, w, _ in groups)
    tm = min(IN_ROW_TILE, seq)
    row = lambda width: pl.BlockSpec((1, tm, width), lambda b, t: (b, t, 0))
    per_batch = pl.BlockSpec((1, 1, d), lambda b, t: (b, 0, 0))
    out_bytes = sum(2 * tm * w * jnp.dtype(dt).itemsize for _, w, dt in groups)
    est = 2 * tm * d * 4 + d * n_tot * 2 + out_bytes + 4 * tm * d * 4 + 6 * tm * 3 * DN_DIM * 4
    return pl.pallas_call(
        functools.partial(_inproj_body, groups),
        grid=(bsz, seq // tm),
        in_specs=[row(d), per_batch, per_batch, _resident((1, d)), _resident((d, n_tot)),
                  _resident((DN_CONV, 3 * DN_DIM))],
        out_specs=[row(w) for _, w, _ in groups],
        out_shape=[jax.ShapeDtypeStruct((bsz, seq, w), dt) for _, w, dt in groups],
        scratch_shapes=[pltpu.VMEM((tm + V7X_SUBLANES, 3 * DN_DIM), F32)],
        compiler_params=pltpu.CompilerParams(
            dimension_semantics=("arbitrary", "arbitrary"), vmem_limit_bytes=_vmem_limit(est)),
        name="in_projection",
    )(x, sc1, sh1, norm_w.reshape(1, d), w_packed, dn_conv_w.astype(F32))


def _attn_body(sink_ref, q_ref, kp_ref, kc_ref, vp_ref, vc_ref, bias_ref, o_ref):
    n = pl.program_id(1)
    n_q = q_ref.shape[1] // WINDOW
    k_all = jnp.concatenate([kp_ref[0], kc_ref[0]], axis=0)
    v_all = jnp.concatenate([vp_ref[0], vc_ref[0]], axis=0)
    qi = lax.broadcasted_iota(jnp.int32, (WINDOW, 2 * WINDOW), 0)
    kj = lax.broadcasted_iota(jnp.int32, (WINDOW, 2 * WINDOW), 1)
    dist = WINDOW + qi - kj
    in_band = (dist >= 0) & (dist < WINDOW)
    first_valid = in_band & ((kj >= WINDOW) | (n > 0))
    units = [(j, h) for j in range(n_q) for h in range(ATTN_Q_HEADS)]
    head_cols = lambda x, i: x[:, i * ATTN_HEAD_DIM:(i + 1) * ATTN_HEAD_DIM]
    band = lambda x, j: x[j * WINDOW:(j + 2) * WINDOW]
    scores = [jnp.where(first_valid if j == 0 else in_band,
                        _dot_nt(head_cols(q_ref[0, j * WINDOW:(j + 1) * WINDOW, :], h),
                                head_cols(band(k_all, j), h // ATTN_GROUP)) + bias_ref[h], NEG_INF)
              for j, h in units]
    maxes = [jnp.maximum(jnp.max(s, axis=-1, keepdims=True), sink_ref[h]) for s, (_, h) in zip(scores, units)]
    probs = [jnp.exp(s - m) for s, m in zip(scores, maxes)]
    denoms = [jnp.sum(p, axis=-1, keepdims=True) + jnp.exp(sink_ref[h] - m)
              for p, m, (_, h) in zip(probs, maxes, units)]
    outs = [_dot(p.astype(BF16), head_cols(band(v_all, j), h // ATTN_GROUP)) / d
            for p, d, (j, h) in zip(probs, denoms, units)]
    for j in range(n_q):
        o_ref[0, j * WINDOW:(j + 1) * WINDOW, :] = jnp.concatenate(
            outs[j * ATTN_Q_HEADS:(j + 1) * ATTN_Q_HEADS], axis=1).astype(o_ref.dtype)


def _attention(q, k, v, sinks, bias):
    bsz, seq, _ = q.shape
    n_q = math.gcd(ATTN_BLOCKS, seq // WINDOW)
    cur = lambda width: pl.BlockSpec((1, n_q * WINDOW, width), lambda b, n: (b, n, 0))
    prev = lambda width: pl.BlockSpec((1, WINDOW, width), lambda b, n: (b, jnp.maximum(n * n_q - 1, 0), 0))
    return pl.pallas_call(
        _attn_body,
        grid=(bsz, seq // (n_q * WINDOW)),
        in_specs=[
            pl.BlockSpec(memory_space=pltpu.SMEM),
            cur(ATTN_Q_DIM), prev(ATTN_KV_DIM), cur(ATTN_KV_DIM), prev(ATTN_KV_DIM), cur(ATTN_KV_DIM),
            _resident((ATTN_Q_HEADS, WINDOW, 2 * WINDOW)),
        ],
        out_specs=cur(ATTN_Q_DIM),
        out_shape=jax.ShapeDtypeStruct((bsz, seq, ATTN_Q_DIM), BF16),
        compiler_params=pltpu.CompilerParams(dimension_semantics=("arbitrary", "arbitrary")),
        name="swa_attention",
    )(sinks.astype(F32), q, k, k, v, v, bias)


def _split3(a):
    hi = a.astype(BF16)
    r = a - hi.astype(F32)
    mid = r.astype(BF16)
    lo = (r - mid.astype(F32)).astype(BF16)
    return hi, mid, lo


def _mm(a, b, dot=_dot):
    return dot(a.astype(BF16), b.astype(BF16))


def _unit_lower_inverses(lowers, ii, jj):
    size = lowers[0].shape[0]
    eye = jnp.where(ii == jj, 1.0, 0.0)
    invs = [eye] * len(lowers)
    s = 1
    while s < size:
        sh = int(math.log2(s))
        joins = (jnp.right_shift(ii, sh + 1) == jnp.right_shift(jj, sh + 1)) & (
            jnp.right_shift(ii, sh) != jnp.right_shift(jj, sh))
        es = [jnp.where(joins, lower, 0.0) for lower in lowers]
        if s == 1:
            invs = [inv - e for inv, e in zip(invs, es)]
        else:
            tmps = [_mm(e, inv) for e, inv in zip(es, invs)]
            invs = [inv - _mm(inv, tmp) for inv, tmp in zip(invs, tmps)]
        s *= 2
    return invs


def _dn_body(qkv_ref, z_ref, bd_ref, alog_ref, dtb_ref, nw_ref, o_ref, state_ref):
    n_b, rows = qkv_ref.shape[0], qkv_ref.shape[1]
    hd = DN_HEAD_DIM

    @pl.when(pl.program_id(1) == 0)
    def _():
        state_ref[...] = jnp.zeros_like(state_ref)

    cc = DN_CHUNK
    n_c = rows // cc
    ri = lax.broadcasted_iota(jnp.int32, (rows, rows), 0)
    ci = lax.broadcasted_iota(jnp.int32, (rows, rows), 1)
    same_chunk = jnp.right_shift(ri, int(math.log2(cc))) == jnp.right_shift(ci, int(math.log2(cc)))
    tri = jnp.where(same_chunk & (ci <= ri), 1.0, 0.0).astype(BF16)
    ii = lax.broadcasted_iota(jnp.int32, (cc, cc), 0)
    jj = lax.broadcasted_iota(jnp.int32, (cc, cc), 1)
    incl = jj <= ii
    strict = jj < ii
    nw = nw_ref[...]

    q_l, k_l, rhs_l, gcc_l, eg_l, gend_l, lower_l, intra_l = [], [], [], [], [], [], [], []
    for b in range(n_b):
        bd = bd_ref[b]
        beta_all = _sigmoid(bd)
        g_all = -jnp.exp(alog_ref[...]) * jax.nn.softplus(bd + dtb_ref[...])
        g_hi, g_mid, g_lo = _split3(g_all)
        gc = _dot(tri, g_hi) + _dot(tri, g_mid) + _dot(tri, g_lo)
        gc_t = gc.T
        eg_all = jnp.exp(gc)
        for h in range(DN_HEADS):
            q_all = qkv_ref[b, :, h * hd:(h + 1) * hd].astype(F32)
            k_all = qkv_ref[b, :, DN_DIM + h * hd:DN_DIM + (h + 1) * hd].astype(F32)
            v_all = qkv_ref[b, :, 2 * DN_DIM + h * hd:2 * DN_DIM + (h + 1) * hd].astype(F32)
            col = DN_HEADS + h
            for c in range(n_c):
                r0 = c * cc
                q = q_all[r0:r0 + cc]
                k = k_all[r0:r0 + cc]
                beta = beta_all[r0:r0 + cc, h:h + 1]
                gcc = gc[r0:r0 + cc, col:col + 1]
                gcr = gc_t[col:col + 1, r0:r0 + cc]
                eg = eg_all[r0:r0 + cc, col:col + 1]
                decay = jnp.where(incl, jnp.exp(jnp.where(incl, gcc - gcr, 0.0)), 0.0)
                kb = k * beta
                q_l.append(q); k_l.append(k); gcc_l.append(gcc); eg_l.append(eg)
                gend_l.append(gc[r0 + cc - 1:r0 + cc, col:col + 1])
                rhs_l.append(jnp.concatenate([v_all[r0:r0 + cc] * beta, kb * eg], axis=1))
                lower_l.append(jnp.where(strict, _mm(kb, k, _dot_nt) * decay, 0.0))
                intra_l.append(jnp.where(incl, _mm(q, k, _dot_nt) * decay, 0.0))
    inv_l = _unit_lower_inverses(lower_l, ii, jj)
    sol_l = [_mm(inv, rhs) for inv, rhs in zip(inv_l, rhs_l)]

    pairs = [(b, h) for b in range(n_b) for h in range(DN_HEADS)]
    unit = lambda b, h, c: (b * DN_HEADS + h) * n_c + c
    state_l = [state_ref[b, h] for b, h in pairs]
    for c in range(n_c):
        r0 = c * cc
        us = [unit(b, h, c) for b, h in pairs]
        vnew_l = [sol_l[u][:, :hd] - _mm(sol_l[u][:, hd:], state) for u, state in zip(us, state_l)]
        o_l = [_mm(q_l[u] * eg_l[u], state) + _mm(intra_l[u], vnew)
               for u, state, vnew in zip(us, state_l, vnew_l)]
        state_l = [state * jnp.exp(gend_l[u]) + _mm(k_l[u] * jnp.exp(gend_l[u] - gcc_l[u]), vnew, _dot_tn)
                   for u, state, vnew in zip(us, state_l, vnew_l)]
        for (b, h), o in zip(pairs, o_l):
            y = _rms(o, nw) * _silu(z_ref[b, r0:r0 + cc, h * hd:(h + 1) * hd].astype(F32))
            o_ref[b, r0:r0 + cc, h * hd:(h + 1) * hd] = y.astype(o_ref.dtype)
    for (b, h), state in zip(pairs, state_l):
        state_ref[b, h] = state


def _deltanet(qkv_act, dz, bd, a_log, dt_bias, norm_w):
    bsz, seq, _ = qkv_act.shape
    rows = min(DN_ROWS, seq)
    n_b = math.gcd(DN_BATCH, bsz)
    lane_row = lambda vec: jnp.pad(vec.astype(F32), (DN_HEADS, BD_PAD - 2 * DN_HEADS)).reshape(1, BD_PAD)
    blk = lambda width: pl.BlockSpec((n_b, rows, width), lambda b, s: (b, s, 0))
    est = n_b * (2 * rows * (3 * DN_DIM * 2 + DN_DIM * 2 + BD_PAD * 4 + DN_DIM * 2) + 6 * rows * 3 * DN_DIM * 4
                 + DN_HEADS * 24 * rows * DN_CHUNK * 4)
    return pl.pallas_call(
        _dn_body,
        grid=(bsz // n_b, seq // rows),
        in_specs=[
            blk(3 * DN_DIM), blk(DN_DIM), blk(BD_PAD),
            _resident((1, BD_PAD)), _resident((1, BD_PAD)), _resident((1, DN_HEAD_DIM)),
        ],
        out_specs=blk(DN_DIM),
        out_shape=jax.ShapeDtypeStruct((bsz, seq, DN_DIM), BF16),
        scratch_shapes=[pltpu.VMEM((n_b, DN_HEADS, DN_HEAD_DIM, DN_HEAD_DIM), F32)],
        compiler_params=pltpu.CompilerParams(
            dimension_semantics=("arbitrary", "arbitrary"), vmem_limit_bytes=_vmem_limit(est)),
        name="gated_deltanet",
    )(qkv_act, dz, bd, lane_row(a_log), lane_row(dt_bias), norm_w.astype(F32).reshape(1, -1))


def _mix_body(x_ref, ya_ref, yd_ref, ga_ref, gd_ref, wab_ref, wdb_ref, wo_ref, npost_ref, g1_ref, npre_ref,
              sc2_ref, sh2_ref, h_ref, u_ref):
    tm = x_ref.shape[1]
    half = tm // SPLIT_ROWS
    groups = [slice(r0, r0 + half) for r0 in range(0, tm, half)]
    branch = [(_dot(ya_ref[0, rs, :], wab_ref[...]), _dot(yd_ref[0, rs, :], wdb_ref[...])) for rs in groups]
    ys = []
    for rs, (ba, bdn) in zip(groups, branch):
        merged = (_sigmoid(ga_ref[0, rs, :].astype(F32)) * ba
                  + _sigmoid(gd_ref[0, rs, :].astype(F32)) * bdn)
        ys.append(_dot(merged.astype(BF16), wo_ref[...]))
    for rs, y in zip(groups, ys):
        h1 = x_ref[0, rs, :] + g1_ref[0] * _rms(y, npost_ref[...])
        h_ref[0, rs, :] = h1
        u_ref[0, rs, :] = (_rms(h1, npre_ref[...]) * (1.0 + sc2_ref[0]) + sh2_ref[0]).astype(u_ref.dtype)


def _mix_out(x, ya, yd, ga, gd, w_ab, w_db, w_out, norm_post, g1, norm_pre, sc2, sh2):
    bsz, seq, d = x.shape
    tm = min(ROW_TILE, seq)
    row = lambda width: pl.BlockSpec((1, tm, width), lambda b, t: (b, t, 0))
    per_batch = pl.BlockSpec((1, 1, d), lambda b, t: (b, 0, 0))
    est = (2 * tm * d * (4 + 2 + 2 + 4 + 2) + 2 * tm * 2 * ATTN_Q_DIM * 2 + (2 * ATTN_Q_DIM + d) * d * 2
           + 6 * tm * d * 4)
    return pl.pallas_call(
        _mix_body,
        grid=(bsz, seq // tm),
        in_specs=[
            row(d), row(ATTN_Q_DIM), row(DN_DIM), row(d), row(d),
            _resident((ATTN_Q_DIM, d)), _resident((DN_DIM, d)), _resident((d, d)),
            _resident((1, d)), per_batch, _resident((1, d)), per_batch, per_batch,
        ],
        out_specs=[row(d), row(d)],
        out_shape=[jax.ShapeDtypeStruct((bsz, seq, d), F32), jax.ShapeDtypeStruct((bsz, seq, d), BF16)],
        compiler_params=pltpu.CompilerParams(
            dimension_semantics=("arbitrary", "arbitrary"), vmem_limit_bytes=_vmem_limit(est)),
        name="merge_out_projection",
    )(x, ya, yd, ga, gd, w_ab, w_db, w_out, norm_post.reshape(1, d), g1, norm_pre.reshape(1, d), sc2, sh2)


def _ffn_body(u_ref, h_ref, wup_ref, cw_ref, wdn_ref, npost_ref, g2_ref, o_ref, gpad_ref, vpad_ref, gcar_ref,
              vcar_ref, act_ref):
    tm = u_ref.shape[1]
    n_ch = D_FF // FF_CHUNK
    sub = V7X_SUBLANES

    @pl.when(pl.program_id(1) == 0)
    def _():
        gcar_ref[...] = jnp.zeros_like(gcar_ref)
        vcar_ref[...] = jnp.zeros_like(vcar_ref)

    u = u_ref[0]

    def conv_half(pad_ref, car_ref, j, off, scale):
        up = _dot(u, wup_ref[:, off:off + FF_CHUNK])
        pad_ref[0:sub, :] = car_ref[j]
        pad_ref[sub:sub + tm, :] = up
        car_ref[j] = up[tm - sub:tm, :]
        taps = cw_ref[:, off:off + FF_CHUNK] * scale
        out = taps[FFN_CONV - 1:FFN_CONV, :] * up
        for t in range(FFN_CONV - 1):
            start = sub - (FFN_CONV - 1) + t
            out = out + taps[t:t + 1, :] * pad_ref[start:start + tm, :]
        return out

    c0 = math.sqrt(2.0 / math.pi)
    for j in range(n_ch):
        gate = conv_half(gpad_ref, gcar_ref, j, j * FF_CHUNK, 1.0)
        half_val = conv_half(vpad_ref, vcar_ref, j, D_FF + j * FF_CHUNK, 0.5)
        t = jnp.tanh(gate * (c0 + (0.044715 * c0) * (gate * gate)))
        act_ref[:, j * FF_CHUNK:(j + 1) * FF_CHUNK] = ((gate + gate * t) * half_val).astype(BF16)

    y = _dot(act_ref[...], wdn_ref[...])
    o_ref[0] = h_ref[0] + g2_ref[0] * _rms(y, npost_ref[...])


def _ffn(u2, h1, w_up, conv_w, w_down, norm_post, g2):
    bsz, seq, d = h1.shape
    tm = min(ROW_TILE, seq)
    n_ch = D_FF // FF_CHUNK
    row = lambda width: pl.BlockSpec((1, tm, width), lambda b, t: (b, t, 0))
    per_batch = pl.BlockSpec((1, 1, d), lambda b, t: (b, 0, 0))
    est = (2 * tm * d * (2 + 4 + 4) + 3 * d * D_FF * 2 + 2 * (tm + 8) * FF_CHUNK * 4 + tm * D_FF * 2
           + 8 * tm * FF_CHUNK * 4 + 2 * tm * d * 4)
    return pl.pallas_call(
        _ffn_body,
        grid=(bsz, seq // tm),
        in_specs=[
            row(d), row(d),
            _resident((d, 2 * D_FF)), _resident((FFN_CONV, 2 * D_FF)), _resident((D_FF, d)),
            _resident((1, d)), per_batch,
        ],
        out_specs=row(d),
        out_shape=jax.ShapeDtypeStruct((bsz, seq, d), F32),
        scratch_shapes=[
            pltpu.VMEM((tm + V7X_SUBLANES, FF_CHUNK), F32),
            pltpu.VMEM((tm + V7X_SUBLANES, FF_CHUNK), F32),
            pltpu.VMEM((n_ch, V7X_SUBLANES, FF_CHUNK), F32),
            pltpu.VMEM((n_ch, V7X_SUBLANES, FF_CHUNK), F32),
            pltpu.VMEM((tm, D_FF), BF16),
        ],
        compiler_params=pltpu.CompilerParams(
            dimension_semantics=("arbitrary", "arbitrary"), vmem_limit_bytes=_vmem_limit(est)),
        name="conv_geglu_ffn",
    )(u2, h1, w_up, conv_w.astype(F32), w_down, norm_post.reshape(1, d), g2)


def _layer(h, c, ada_w, ada_b, norm_mix_pre, norm_mix_post, norm_ffn_pre, norm_ffn_post, w_in, dn_conv_w,
           dn_a_log, dn_dt_bias, dn_norm_w, attn_sinks, bias, w_attn_branch, w_dn_branch, w_out, ffn_w_up,
           ffn_conv_w, ffn_w_down):
    bsz, seq, d = h.shape
    mod = _modulation(c, ada_w, ada_b)
    sh1, sc1, g1, sh2, sc2, g2 = [mod[:, i * d:(i + 1) * d].reshape(bsz, 1, d) for i in range(N_MOD)]
    q, k, v, dqkv, dz, ga, gd, bd = _in_projection(h, sc1, sh1, norm_mix_pre, _pack_w_in(w_in), dn_conv_w)
    y_attn = _attention(q, k, v, attn_sinks, bias)
    y_dn = _deltanet(dqkv, dz, bd, dn_a_log, dn_dt_bias, dn_norm_w)
    h1, u2 = _mix_out(h, y_attn, y_dn, ga, gd, w_attn_branch.astype(BF16), w_dn_branch.astype(BF16),
                      w_out.astype(BF16), norm_mix_post, g1, norm_ffn_pre, sc2, sh2)
    return _ffn(u2, h1, ffn_w_up.astype(BF16), ffn_conv_w, ffn_w_down.astype(BF16), norm_ffn_post, g2)


def kernel(x, c, ada_w, ada_b, norm_mix_pre, norm_mix_post, norm_ffn_pre, norm_ffn_post, w_in, dn_conv_w, dn_a_log, dn_dt_bias, dn_norm_w, attn_sinks, rel_bias, w_attn_branch, w_dn_branch, w_out, ffn_w_up, ffn_conv_w, ffn_w_down):
    bias = _rel_bias_table(rel_bias)
    h = x
    for l in range(ada_w.shape[0]):
        h = _layer(h, c, ada_w[l], ada_b[l], norm_mix_pre[l], norm_mix_post[l], norm_ffn_pre[l],
                   norm_ffn_post[l], w_in[l], dn_conv_w[l], dn_a_log[l], dn_dt_bias[l], dn_norm_w[l],
                   attn_sinks[l], bias, w_attn_branch[l], w_dn_branch[l], w_out[l], ffn_w_up[l],
                   ffn_conv_w[l], ffn_w_down[l])
    return h
```

```python
import functools
import math

import numpy as np
import jax
import jax.numpy as jnp
from jax import lax
from jax.experimental import pallas as pl
from jax.experimental.pallas import tpu as pltpu

ATTN_Q_HEADS = 8
ATTN_KV_HEADS = 2
ATTN_HEAD_DIM = 64
WINDOW = 128
REL_BUCKETS = 32
REL_MAX_DIST = 128
DN_HEADS = 4
DN_HEAD_DIM = 128
DN_CONV = 4
DN_CHUNK = 128
D_FF = 2816
FFN_CONV = 3
RMS_EPS = 1e-6
L2_EPS = 1e-6
N_MOD = 6
NEG_INF = -1e30

ATTN_Q_DIM = ATTN_Q_HEADS * ATTN_HEAD_DIM
ATTN_KV_DIM = ATTN_KV_HEADS * ATTN_HEAD_DIM
DN_DIM = DN_HEADS * DN_HEAD_DIM
ATTN_GROUP = ATTN_Q_HEADS // ATTN_KV_HEADS

V7X_LANES = 128
V7X_SUBLANES = 8
V7X_VMEM_BYTES = 64 * 1024 * 1024

F32 = jnp.float32
BF16 = jnp.bfloat16

ROW_TILE = 1024
IN_ROW_TILE = 512
DN_ACT_COLS = 256
SPLIT_ROWS = 4
ATTN_BLOCKS = 16
DN_ROWS = 256
DN_BATCH = 2
FF_CHUNK = 256
BD_PAD = V7X_LANES


def _vmem_limit(nbytes):
    return int(min(V7X_VMEM_BYTES, nbytes + nbytes // 2 + (8 << 20)))


def _resident(shape):
    nd = len(shape)
    return pl.BlockSpec(shape, lambda *_: (0,) * nd, pipeline_mode=pl.Buffered(1))


def _dot(a, b):
    return jnp.dot(a, b, preferred_element_type=F32)


def _dot_nt(a, b):
    return lax.dot_general(a, b, (((1,), (1,)), ((), ())), preferred_element_type=F32)


def _dot_tn(a, b):
    return lax.dot_general(a, b, (((0,), (0,)), ((), ())), preferred_element_type=F32)


def _rms(xf, w):
    return xf * lax.rsqrt(jnp.mean(xf * xf, axis=-1, keepdims=True) + RMS_EPS) * w


def _sigmoid(x):
    return 1.0 / (1.0 + jnp.exp(-x))


def _silu(x):
    return x * _sigmoid(x)


def _mod_body(c_ref, w_ref, b_ref, o_ref):
    ca = _silu(c_ref[...])
    o_ref[...] = _dot(ca.astype(BF16), w_ref[...].astype(BF16)) + b_ref[...]


def _modulation(c, ada_w, ada_b):
    bsz, d = c.shape
    n = ada_w.shape[1]
    tn = d
    return pl.pallas_call(
        _mod_body,
        grid=(n // tn,),
        in_specs=[
            pl.BlockSpec((bsz, d), lambda j: (0, 0)),
            pl.BlockSpec((d, tn), lambda j: (0, j)),
            pl.BlockSpec((1, tn), lambda j: (0, j)),
        ],
        out_specs=pl.BlockSpec((bsz, tn), lambda j: (0, j)),
        out_shape=jax.ShapeDtypeStruct((bsz, n), F32),
        compiler_params=pltpu.CompilerParams(
            dimension_semantics=("arbitrary",), vmem_limit_bytes=_vmem_limit(2 * d * tn * 4)),
        name="adaln_modulation",
    )(c, ada_w, ada_b.reshape(1, n))


def _t5_bucket_band():
    qi = np.arange(WINDOW)[:, None]
    kj = np.arange(2 * WINDOW)[None, :]
    dist = np.maximum(WINDOW + qi - kj, 0)
    max_exact = REL_BUCKETS // 2
    scaled = np.log(np.maximum(dist, 1).astype(np.float32) / np.float32(max_exact)) / np.float32(
        math.log(REL_MAX_DIST / max_exact))
    large = max_exact + (scaled.astype(np.float32) * np.float32(REL_BUCKETS - max_exact)).astype(np.int32)
    large = np.minimum(large, REL_BUCKETS - 1)
    return np.where(dist < max_exact, dist, large).astype(np.int32)


def _bias_body(rb_ref, bucket_ref, o_ref):
    h = pl.program_id(0)
    bucket = bucket_ref[...]
    acc = jnp.zeros(bucket.shape, F32)
    for v in range(REL_BUCKETS):
        acc = jnp.where(bucket == v, rb_ref[v, h], acc)
    o_ref[0] = acc


def _rel_bias_table(rel_bias):
    bucket = jnp.asarray(_t5_bucket_band())
    return pl.pallas_call(
        _bias_body,
        grid=(ATTN_Q_HEADS,),
        in_specs=[
            pl.BlockSpec(memory_space=pltpu.SMEM),
            pl.BlockSpec((WINDOW, 2 * WINDOW), lambda h: (0, 0)),
        ],
        out_specs=pl.BlockSpec((1, WINDOW, 2 * WINDOW), lambda h: (h, 0, 0)),
        out_shape=jax.ShapeDtypeStruct((ATTN_Q_HEADS, WINDOW, 2 * WINDOW), F32),
        compiler_params=pltpu.CompilerParams(dimension_semantics=("arbitrary",)),
        name="rel_bias_table",
    )(rel_bias.astype(F32), bucket)


_IN_GROUPS = (
    ("q", ATTN_Q_DIM, BF16),
    ("k", ATTN_KV_DIM, BF16),
    ("v", ATTN_KV_DIM, BF16),
    ("dqkv", 3 * DN_DIM, BF16),
    ("dz", DN_DIM, BF16),
    ("ga", None, BF16),
    ("gd", None, BF16),
    ("bd", BD_PAD, F32),
)


def _in_groups(d_model):
    return tuple((n, d_model if w is None else w, dt) for n, w, dt in _IN_GROUPS)


def _pack_w_in(w_in):
    d = w_in.shape[0]
    o = 0
    aq = w_in[:, o:o + ATTN_Q_DIM] * (ATTN_HEAD_DIM ** -0.5); o += ATTN_Q_DIM
    ak = w_in[:, o:o + ATTN_KV_DIM]; o += ATTN_KV_DIM
    av = w_in[:, o:o + ATTN_KV_DIM]; o += ATTN_KV_DIM
    dqkv = w_in[:, o:o + 3 * DN_DIM]; o += 3 * DN_DIM
    dz = w_in[:, o:o + DN_DIM]; o += DN_DIM
    bd = w_in[:, o:o + 2 * DN_HEADS]; o += 2 * DN_HEADS
    ga = w_in[:, o:o + d]; o += d
    gd = w_in[:, o:o + d]; o += d
    bd = jnp.pad(bd, ((0, 0), (0, BD_PAD - 2 * DN_HEADS)))
    return jnp.concatenate([aq, ak, av, dqkv, dz, ga, gd, bd], axis=1).astype(BF16)


def _dn_activate(y, lo, cw_ref, xpad_ref, o_ref):
    rows, width = y.shape
    sub = V7X_SUBLANES
    cols = slice(lo, lo + width)
    xpad_ref[sub:sub + rows, cols] = y
    acc = cw_ref[DN_CONV - 1:DN_CONV, cols] * y
    for j in range(DN_CONV - 1):
        start = sub - (DN_CONV - 1) + j
        acc = acc + cw_ref[j:j + 1, cols] * xpad_ref[start:start + rows, cols]
    xpad_ref[0:sub, cols] = y[rows - sub:rows, :]
    a = _silu(acc)
    if lo < 2 * DN_DIM:
        heads = []
        for h in range(width // DN_HEAD_DIM):
            t = a[:, h * DN_HEAD_DIM:(h + 1) * DN_HEAD_DIM]
            t = t * lax.rsqrt(jnp.sum(t * t, axis=-1, keepdims=True) + L2_EPS)
            heads.append(t * (DN_HEAD_DIM ** -0.5) if lo < DN_DIM else t)
        a = jnp.concatenate(heads, axis=1)
    o_ref[0, :, cols] = a.astype(o_ref.dtype)


def _inproj_body(groups, x_ref, sc_ref, sh_ref, nw_ref, w_ref, cw_ref, *refs):
    out_refs, xpad_ref = refs[:-1], refs[-1]

    @pl.when(pl.program_id(1) == 0)
    def _():
        xpad_ref[0:V7X_SUBLANES, :] = jnp.zeros((V7X_SUBLANES, xpad_ref.shape[1]), F32)

    xf = x_ref[0]
    u = (_rms(xf, nw_ref[...]) * (1.0 + sc_ref[0]) + sh_ref[0]).astype(BF16)
    offs = np.cumsum([0] + [w for _, w, _ in groups])
    order = sorted(range(len(groups)), key=lambda i: groups[i][0] != "dqkv")
    for i in order:
        (name, width, dt), o_ref, off = groups[i], out_refs[i], int(offs[i])
        if name == "dqkv":
            for lo in range(0, width, DN_ACT_COLS):
                _dn_activate(_dot(u, w_ref[:, off + lo:off + lo + DN_ACT_COLS]), lo, cw_ref, xpad_ref, o_ref)
        else:
            o_ref[0] = _dot(u, w_ref[:, off:off + width]).astype(dt)


def _in_projection(x, sc1, sh1, norm_w, w_packed, dn_conv_w):
    bsz, seq, d = x.shape
    groups = _in_groups(d)
    n_tot = sum(w for _, w, _ in groups)
    tm = min(IN_ROW_TILE, seq)
    row = lambda width: pl.BlockSpec((1, tm, width), lambda b, t: (b, t, 0))
    per_batch = pl.BlockSpec((1, 1, d), lambda b, t: (b, 0, 0))
    out_bytes = sum(2 * tm * w * jnp.dtype(dt).itemsize for _, w, dt in groups)
    est = 2 * tm * d * 4 + d * n_tot * 2 + out_bytes + 4 * tm * d * 4 + 6 * tm * 3 * DN_DIM * 4
    return pl.pallas_call(
        functools.partial(_inproj_body, groups),
        grid=(bsz, seq // tm),
        in_specs=[row(d), per_batch, per_batch, _resident((1, d)), _resident((d, n_tot)),
                  _resident((DN_CONV, 3 * DN_DIM))],
        out_specs=[row(w) for _, w, _ in groups],
        out_shape=[jax.ShapeDtypeStruct((bsz, seq, w), dt) for _, w, dt in groups],
        scratch_shapes=[pltpu.VMEM((tm + V7X_SUBLANES, 3 * DN_DIM), F32)],
        compiler_params=pltpu.CompilerParams(
            dimension_semantics=("arbitrary", "arbitrary"), vmem_limit_bytes=_vmem_limit(est)),
        name="in_projection",
    )(x, sc1, sh1, norm_w.reshape(1, d), w_packed, dn_conv_w.astype(F32))


def _attn_body(sink_ref, q_ref, kp_ref, kc_ref, vp_ref, vc_ref, bias_ref, o_ref):
    n = pl.program_id(1)
    n_q = q_ref.shape[1] // WINDOW
    k_all = jnp.concatenate([kp_ref[0], kc_ref[0]], axis=0)
    v_all = jnp.concatenate([vp_ref[0], vc_ref[0]], axis=0)
    qi = lax.broadcasted_iota(jnp.int32, (WINDOW, 2 * WINDOW), 0)
    kj = lax.broadcasted_iota(jnp.int32, (WINDOW, 2 * WINDOW), 1)
    dist = WINDOW + qi - kj
    in_band = (dist >= 0) & (dist < WINDOW)
    first_valid = in_band & ((kj >= WINDOW) | (n > 0))
    units = [(j, h) for j in range(n_q) for h in range(ATTN_Q_HEADS)]
    head_cols = lambda x, i: x[:, i * ATTN_HEAD_DIM:(i + 1) * ATTN_HEAD_DIM]
    band = lambda x, j: x[j * WINDOW:(j + 2) * WINDOW]
    scores = [jnp.where(first_valid if j == 0 else in_band,
                        _dot_nt(head_cols(q_ref[0, j * WINDOW:(j + 1) * WINDOW, :], h),
                                head_cols(band(k_all, j), h // ATTN_GROUP)) + bias_ref[h], NEG_INF)
              for j, h in units]
    maxes = [jnp.maximum(jnp.max(s, axis=-1, keepdims=True), sink_ref[h]) for s, (_, h) in zip(scores, units)]
    probs = [jnp.exp(s - m) for s, m in zip(scores, maxes)]
    denoms = [jnp.sum(p, axis=-1, keepdims=True) + jnp.exp(sink_ref[h] - m)
              for p, m, (_, h) in zip(probs, maxes, units)]
    outs = [_dot(p.astype(BF16), head_cols(band(v_all, j), h // ATTN_GROUP)) / d
            for p, d, (j, h) in zip(probs, denoms, units)]
    for j in range(n_q):
        o_ref[0, j * WINDOW:(j + 1) * WINDOW, :] = jnp.concatenate(
            outs[j * ATTN_Q_HEADS:(j + 1) * ATTN_Q_HEADS], axis=1).astype(o_ref.dtype)


def _attention(q, k, v, sinks, bias):
    bsz, seq, _ = q.shape
    n_q = math.gcd(ATTN_BLOCKS, seq // WINDOW)
    cur = lambda width: pl.BlockSpec((1, n_q * WINDOW, width), lambda b, n: (b, n, 0))
    prev = lambda width: pl.BlockSpec((1, WINDOW, width), lambda b, n: (b, jnp.maximum(n * n_q - 1, 0), 0))
    return pl.pallas_call(
        _attn_body,
        grid=(bsz, seq // (n_q * WINDOW)),
        in_specs=[
            pl.BlockSpec(memory_space=pltpu.SMEM),
            cur(ATTN_Q_DIM), prev(ATTN_KV_DIM), cur(ATTN_KV_DIM), prev(ATTN_KV_DIM), cur(ATTN_KV_DIM),
            _resident((ATTN_Q_HEADS, WINDOW, 2 * WINDOW)),
        ],
        out_specs=cur(ATTN_Q_DIM),
        out_shape=jax.ShapeDtypeStruct((bsz, seq, ATTN_Q_DIM), BF16),
        compiler_params=pltpu.CompilerParams(dimension_semantics=("arbitrary", "arbitrary")),
        name="swa_attention",
    )(sinks.astype(F32), q, k, k, v, v, bias)


def _split3(a):
    hi = a.astype(BF16)
    r = a - hi.astype(F32)
    mid = r.astype(BF16)
    lo = (r - mid.astype(F32)).astype(BF16)
    return hi, mid, lo


def _mm(a, b, dot=_dot):
    return dot(a.astype(BF16), b.astype(BF16))


def _unit_lower_inverses(lowers, ii, jj):
    size = lowers[0].shape[0]
    eye = jnp.where(ii == jj, 1.0, 0.0)
    invs = [eye] * len(lowers)
    s = 1
    while s < size:
        sh = int(math.log2(s))
        joins = (jnp.right_shift(ii, sh + 1) == jnp.right_shift(jj, sh + 1)) & (
            jnp.right_shift(ii, sh) != jnp.right_shift(jj, sh))
        es = [jnp.where(joins, lower, 0.0) for lower in lowers]
        if s == 1:
            invs = [inv - e for inv, e in zip(invs, es)]
        else:
            tmps = [_mm(e, inv) for e, inv in zip(es, invs)]
            invs = [inv - _mm(inv, tmp) for inv, tmp in zip(invs, tmps)]
        s *= 2
    return invs


def _dn_body(qkv_ref, z_ref, bd_ref, alog_ref, dtb_ref, nw_ref, o_ref, state_ref):
    n_b, rows = qkv_ref.shape[0], qkv_ref.shape[1]
    hd = DN_HEAD_DIM

    @pl.when(pl.program_id(1) == 0)
    def _():
        state_ref[...] = jnp.zeros_like(state_ref)

    cc = DN_CHUNK
    n_c = rows // cc
    ri = lax.broadcasted_iota(jnp.int32, (rows, rows), 0)
    ci = lax.broadcasted_iota(jnp.int32, (rows, rows), 1)
    same_chunk = jnp.right_shift(ri, int(math.log2(cc))) == jnp.right_shift(ci, int(math.log2(cc)))
    tri = jnp.where(same_chunk & (ci <= ri), 1.0, 0.0).astype(BF16)
    ii = lax.broadcasted_iota(jnp.int32, (cc, cc), 0)
    jj = lax.broadcasted_iota(jnp.int32, (cc, cc), 1)
    incl = jj <= ii
    strict = jj < ii
    nw = nw_ref[...]

    q_l, k_l, rhs_l, gcc_l, eg_l, gend_l, lower_l, intra_l = [], [], [], [], [], [], [], []
    for b in range(n_b):
        bd = bd_ref[b]
        beta_all = _sigmoid(bd)
        g_all = -jnp.exp(alog_ref[...]) * jax.nn.softplus(bd + dtb_ref[...])
        g_hi, g_mid, g_lo = _split3(g_all)
        gc = _dot(tri, g_hi) + _dot(tri, g_mid) + _dot(tri, g_lo)
        gc_t = gc.T
        eg_all = jnp.exp(gc)
        for h in range(DN_HEADS):
            q_all = qkv_ref[b, :, h * hd:(h + 1) * hd].astype(F32)
            k_all = qkv_ref[b, :, DN_DIM + h * hd:DN_DIM + (h + 1) * hd].astype(F32)
            v_all = qkv_ref[b, :, 2 * DN_DIM + h * hd:2 * DN_DIM + (h + 1) * hd].astype(F32)
            col = DN_HEADS + h
            for c in range(n_c):
                r0 = c * cc
                q = q_all[r0:r0 + cc]
                k = k_all[r0:r0 + cc]
                beta = beta_all[r0:r0 + cc, h:h + 1]
                gcc = gc[r0:r0 + cc, col:col + 1]
                gcr = gc_t[col:col + 1, r0:r0 + cc]
                eg = eg_all[r0:r0 + cc, col:col + 1]
                decay = jnp.where(incl, jnp.exp(jnp.where(incl, gcc - gcr, 0.0)), 0.0)
                kb = k * beta
                q_l.append(q); k_l.append(k); gcc_l.append(gcc); eg_l.append(eg)
                gend_l.append(gc[r0 + cc - 1:r0 + cc, col:col + 1])
                rhs_l.append(jnp.concatenate([v_all[r0:r0 + cc] * beta, kb * eg], axis=1))
                lower_l.append(jnp.where(strict, _mm(kb, k, _dot_nt) * decay, 0.0))
                intra_l.append(jnp.where(incl, _mm(q, k, _dot_nt) * decay, 0.0))
    inv_l = _unit_lower_inverses(lower_l, ii, jj)
    sol_l = [_mm(inv, rhs) for inv, rhs in zip(inv_l, rhs_l)]

    pairs = [(b, h) for b in range(n_b) for h in range(DN_HEADS)]
    unit = lambda b, h, c: (b * DN_HEADS + h) * n_c + c
    state_l = [state_ref[b, h] for b, h in pairs]
    for c in range(n_c):
        r0 = c * cc
        us = [unit(b, h, c) for b, h in pairs]
        vnew_l = [sol_l[u][:, :hd] - _mm(sol_l[u][:, hd:], state) for u, state in zip(us, state_l)]
        o_l = [_mm(q_l[u] * eg_l[u], state) + _mm(intra_l[u], vnew)
               for u, state, vnew in zip(us, state_l, vnew_l)]
        state_l = [state * jnp.exp(gend_l[u]) + _mm(k_l[u] * jnp.exp(gend_l[u] - gcc_l[u]), vnew, _dot_tn)
                   for u, state, vnew in zip(us, state_l, vnew_l)]
        for (b, h), o in zip(pairs, o_l):
            y = _rms(o, nw) * _silu(z_ref[b, r0:r0 + cc, h * hd:(h + 1) * hd].astype(F32))
            o_ref[b, r0:r0 + cc, h * hd:(h + 1) * hd] = y.astype(o_ref.dtype)
    for (b, h), state in zip(pairs, state_l):
        state_ref[b, h] = state


def _deltanet(qkv_act, dz, bd, a_log, dt_bias, norm_w):
    bsz, seq, _ = qkv_act.shape
    rows = min(DN_ROWS, seq)
    n_b = math.gcd(DN_BATCH, bsz)
    lane_row = lambda vec: jnp.pad(vec.astype(F32), (DN_HEADS, BD_PAD - 2 * DN_HEADS)).reshape(1, BD_PAD)
    blk = lambda width: pl.BlockSpec((n_b, rows, width), lambda b, s: (b, s, 0))
    est = n_b * (2 * rows * (3 * DN_DIM * 2 + DN_DIM * 2 + BD_PAD * 4 + DN_DIM * 2) + 6 * rows * 3 * DN_DIM * 4
                 + DN_HEADS * 24 * rows * DN_CHUNK * 4)
    return pl.pallas_call(
        _dn_body,
        grid=(bsz // n_b, seq // rows),
        in_specs=[
            blk(3 * DN_DIM), blk(DN_DIM), blk(BD_PAD),
            _resident((1, BD_PAD)), _resident((1, BD_PAD)), _resident((1, DN_HEAD_DIM)),
        ],
        out_specs=blk(DN_DIM),
        out_shape=jax.ShapeDtypeStruct((bsz, seq, DN_DIM), BF16),
        scratch_shapes=[pltpu.VMEM((n_b, DN_HEADS, DN_HEAD_DIM, DN_HEAD_DIM), F32)],
        compiler_params=pltpu.CompilerParams(
            dimension_semantics=("arbitrary", "arbitrary"), vmem_limit_bytes=_vmem_limit(est)),
        name="gated_deltanet",
    )(qkv_act, dz, bd, lane_row(a_log), lane_row(dt_bias), norm_w.astype(F32).reshape(1, -1))


def _mix_body(x_ref, ya_ref, yd_ref, ga_ref, gd_ref, wab_ref, wdb_ref, wo_ref, npost_ref, g1_ref, npre_ref,
              sc2_ref, sh2_ref, h_ref, u_ref):
    tm = x_ref.shape[1]
    half = tm // SPLIT_ROWS
    groups = [slice(r0, r0 + half) for r0 in range(0, tm, half)]
    branch = [(_dot(ya_ref[0, rs, :], wab_ref[...]), _dot(yd_ref[0, rs, :], wdb_ref[...])) for rs in groups]
    ys = []
    for rs, (ba, bdn) in zip(groups, branch):
        merged = (_sigmoid(ga_ref[0, rs, :].astype(F32)) * ba
                  + _sigmoid(gd_ref[0, rs, :].astype(F32)) * bdn)
        ys.append(_dot(merged.astype(BF16), wo_ref[...]))
    for rs, y in zip(groups, ys):
        h1 = x_ref[0, rs, :] + g1_ref[0] * _rms(y, npost_ref[...])
        h_ref[0, rs, :] = h1
        u_ref[0, rs, :] = (_rms(h1, npre_ref[...]) * (1.0 + sc2_ref[0]) + sh2_ref[0]).astype(u_ref.dtype)


def _mix_out(x, ya, yd, ga, gd, w_ab, w_db, w_out, norm_post, g1, norm_pre, sc2, sh2):
    bsz, seq, d = x.shape
    tm = min(ROW_TILE, seq)
    row = lambda width: pl.BlockSpec((1, tm, width), lambda b, t: (b, t, 0))
    per_batch = pl.BlockSpec((1, 1, d), lambda b, t: (b, 0, 0))
    est = (2 * tm * d * (4 + 2 + 2 + 4 + 2) + 2 * tm * 2 * ATTN_Q_DIM * 2 + (2 * ATTN_Q_DIM + d) * d * 2
           + 6 * tm * d * 4)
    return pl.pallas_call(
        _mix_body,
        grid=(bsz, seq // tm),
        in_specs=[
            row(d), row(ATTN_Q_DIM), row(DN_DIM), row(d), row(d),
            _resident((ATTN_Q_DIM, d)), _resident((DN_DIM, d)), _resident((d, d)),
            _resident((1, d)), per_batch, _resident((1, d)), per_batch, per_batch,
        ],
        out_specs=[row(d), row(d)],
        out_shape=[jax.ShapeDtypeStruct((bsz, seq, d), F32), jax.ShapeDtypeStruct((bsz, seq, d), BF16)],
        compiler_params=pltpu.CompilerParams(
            dimension_semantics=("arbitrary", "arbitrary"), vmem_limit_bytes=_vmem_limit(est)),
        name="merge_out_projection",
    )(x, ya, yd, ga, gd, w_ab, w_db, w_out, norm_post.reshape(1, d), g1, norm_pre.reshape(1, d), sc2, sh2)


def _ffn_body(u_ref, h_ref, wup_ref, cw_ref, wdn_ref, npost_ref, g2_ref, o_ref, gpad_ref, vpad_ref, gcar_ref,
              vcar_ref, act_ref):
    tm = u_ref.shape[1]
    n_ch = D_FF // FF_CHUNK
    sub = V7X_SUBLANES

    @pl.when(pl.program_id(1) == 0)
    def _():
        gcar_ref[...] = jnp.zeros_like(gcar_ref)
        vcar_ref[...] = jnp.zeros_like(vcar_ref)

    u = u_ref[0]

    def conv_half(pad_ref, car_ref, j, off, scale):
        up = _dot(u, wup_ref[:, off:off + FF_CHUNK])
        pad_ref[0:sub, :] = car_ref[j]
        pad_ref[sub:sub + tm, :] = up
        car_ref[j] = up[tm - sub:tm, :]
        taps = cw_ref[:, off:off + FF_CHUNK] * scale
        out = taps[FFN_CONV - 1:FFN_CONV, :] * up
        for t in range(FFN_CONV - 1):
            start = sub - (FFN_CONV - 1) + t
            out = out + taps[t:t + 1, :] * pad_ref[start:start + tm, :]
        return out

    c0 = math.sqrt(2.0 / math.pi)
    for j in range(n_ch):
        gate = conv_half(gpad_ref, gcar_ref, j, j * FF_CHUNK, 1.0)
        half_val = conv_half(vpad_ref, vcar_ref, j, D_FF + j * FF_CHUNK, 0.5)
        t = jnp.tanh(gate * (c0 + (0.044715 * c0) * (gate * gate)))
        act_ref[:, j * FF_CHUNK:(j + 1) * FF_CHUNK] = ((gate + gate * t) * half_val).astype(BF16)

    y = _dot(act_ref[...], wdn_ref[...])
    o_ref[0] = h_ref[0] + g2_ref[0] * _rms(y, npost_ref[...])


def _ffn(u2, h1, w_up, conv_w, w_down, norm_post, g2):
    bsz, seq, d = h1.shape
    tm = min(ROW_TILE, seq)
    n_ch = D_FF // FF_CHUNK
    row = lambda width: pl.BlockSpec((1, tm, width), lambda b, t: (b, t, 0))
    per_batch = pl.BlockSpec((1, 1, d), lambda b, t: (b, 0, 0))
    est = (2 * tm * d * (2 + 4 + 4) + 3 * d * D_FF * 2 + 2 * (tm + 8) * FF_CHUNK * 4 + tm * D_FF * 2
           + 8 * tm * FF_CHUNK * 4 + 2 * tm * d * 4)
    return pl.pallas_call(
        _ffn_body,
        grid=(bsz, seq // tm),
        in_specs=[
            row(d), row(d),
            _resident((d, 2 * D_FF)), _resident((FFN_CONV, 2 * D_FF)), _resident((D_FF, d)),
            _resident((1, d)), per_batch,
        ],
        out_specs=row(d),
        out_shape=jax.ShapeDtypeStruct((bsz, seq, d), F32),
        scratch_shapes=[
            pltpu.VMEM((tm + V7X_SUBLANES, FF_CHUNK), F32),
            pltpu.VMEM((tm + V7X_SUBLANES, FF_CHUNK), F32),
            pltpu.VMEM((n_ch, V7X_SUBLANES, FF_CHUNK), F32),
            pltpu.VMEM((n_ch, V7X_SUBLANES, FF_CHUNK), F32),
            pltpu.VMEM((tm, D_FF), BF16),
        ],
        compiler_params=pltpu.CompilerParams(
            dimension_semantics=("arbitrary", "arbitrary"), vmem_limit_bytes=_vmem_limit(est)),
        name="conv_geglu_ffn",
    )(u2, h1, w_up, conv_w.astype(F32), w_down, norm_post.reshape(1, d), g2)


def _layer(h, c, ada_w, ada_b, norm_mix_pre, norm_mix_post, norm_ffn_pre, norm_ffn_post, w_in, dn_conv_w,
           dn_a_log, dn_dt_bias, dn_norm_w, attn_sinks, bias, w_attn_branch, w_dn_branch, w_out, ffn_w_up,
           ffn_conv_w, ffn_w_down):
    bsz, seq, d = h.shape
    mod = _modulation(c, ada_w, ada_b)
    sh1, sc1, g1, sh2, sc2, g2 = [mod[:, i * d:(i + 1) * d].reshape(bsz, 1, d) for i in range(N_MOD)]
    q, k, v, dqkv, dz, ga, gd, bd = _in_projection(h, sc1, sh1, norm_mix_pre, _pack_w_in(w_in), dn_conv_w)
    y_attn = _attention(q, k, v, attn_sinks, bias)
    y_dn = _deltanet(dqkv, dz, bd, dn_a_log, dn_dt_bias, dn_norm_w)
    h1, u2 = _mix_out(h, y_attn, y_dn, ga, gd, w_attn_branch.astype(BF16), w_dn_branch.astype(BF16),
                      w_out.astype(BF16), norm_mix_post, g1, norm_ffn_pre, sc2, sh2)
    return _ffn(u2, h1, ffn_w_up.astype(BF16), ffn_conv_w, ffn_w_down.astype(BF16), norm_ffn_post, g2)


def kernel(x, c, ada_w, ada_b, norm_mix_pre, norm_mix_post, norm_ffn_pre, norm_ffn_post, w_in, dn_conv_w, dn_a_log, dn_dt_bias, dn_norm_w, attn_sinks, rel_bias, w_attn_branch, w_dn_branch, w_out, ffn_w_up, ffn_conv_w, ffn_w_down):
    bias = _rel_bias_table(rel_bias)
    h = x
    for l in range(ada_w.shape[0]):
        h = _layer(h, c, ada_w[l], ada_b[l], norm_mix_pre[l], norm_mix_post[l], norm_ffn_pre[l],
                   norm_ffn_post[l], w_in[l], dn_conv_w[l], dn_a_log[l], dn_dt_bias[l], dn_norm_w[l],
                   attn_sinks[l], bias, w_attn_branch[l], w_dn_branch[l], w_out[l], ffn_w_up[l],
                   ffn_conv_w[l], ffn_w_down[l])
    return h
```

```python
import functools
import math

import numpy as np
import jax
import jax.numpy as jnp
from jax import lax
from jax.experimental import pallas as pl
from jax.experimental.pallas import tpu as pltpu

ATTN_Q_HEADS = 8
ATTN_KV_HEADS = 2
ATTN_HEAD_DIM = 64
WINDOW = 128
REL_BUCKETS = 32
REL_MAX_DIST = 128
DN_HEADS = 4
DN_HEAD_DIM = 128
DN_CONV = 4
DN_CHUNK = 128
D_FF = 2816
FFN_CONV = 3
RMS_EPS = 1e-6
L2_EPS = 1e-6
N_MOD = 6
NEG_INF = -1e30

ATTN_Q_DIM = ATTN_Q_HEADS * ATTN_HEAD_DIM
ATTN_KV_DIM = ATTN_KV_HEADS * ATTN_HEAD_DIM
DN_DIM = DN_HEADS * DN_HEAD_DIM
ATTN_GROUP = ATTN_Q_HEADS // ATTN_KV_HEADS

V7X_LANES = 128
V7X_SUBLANES = 8
V7X_VMEM_BYTES = 64 * 1024 * 1024

F32 = jnp.float32
BF16 = jnp.bfloat16

ROW_TILE = 1024
IN_ROW_TILE = 1024
DN_ACT_COLS = 256
SPLIT_ROWS = 4
ATTN_BLOCKS = 16
DN_ROWS = 256
DN_BATCH = 2
FF_CHUNK = 256
BD_PAD = V7X_LANES


def _vmem_limit(nbytes):
    return int(min(V7X_VMEM_BYTES, nbytes + nbytes // 2 + (8 << 20)))


def _resident(shape):
    nd = len(shape)
    return pl.BlockSpec(shape, lambda *_: (0,) * nd, pipeline_mode=pl.Buffered(1))


def _dot(a, b):
    return jnp.dot(a, b, preferred_element_type=F32)


def _dot_nt(a, b):
    return lax.dot_general(a, b, (((1,), (1,)), ((), ())), preferred_element_type=F32)


def _dot_tn(a, b):
    return lax.dot_general(a, b, (((0,), (0,)), ((), ())), preferred_element_type=F32)


def _rms(xf, w):
    return xf * lax.rsqrt(jnp.mean(xf * xf, axis=-1, keepdims=True) + RMS_EPS) * w


def _sigmoid(x):
    return 1.0 / (1.0 + jnp.exp(-x))


def _silu(x):
    return x * _sigmoid(x)


def _mod_body(c_ref, w_ref, b_ref, o_ref):
    ca = _silu(c_ref[...])
    o_ref[...] = _dot(ca.astype(BF16), w_ref[...].astype(BF16)) + b_ref[...]


def _modulation(c, ada_w, ada_b):
    bsz, d = c.shape
    n = ada_w.shape[1]
    tn = d
    return pl.pallas_call(
        _mod_body,
        grid=(n // tn,),
        in_specs=[
            pl.BlockSpec((bsz, d), lambda j: (0, 0)),
            pl.BlockSpec((d, tn), lambda j: (0, j)),
            pl.BlockSpec((1, tn), lambda j: (0, j)),
        ],
        out_specs=pl.BlockSpec((bsz, tn), lambda j: (0, j)),
        out_shape=jax.ShapeDtypeStruct((bsz, n), F32),
        compiler_params=pltpu.CompilerParams(
            dimension_semantics=("arbitrary",), vmem_limit_bytes=_vmem_limit(2 * d * tn * 4)),
        name="adaln_modulation",
    )(c, ada_w, ada_b.reshape(1, n))


def _t5_bucket_band():
    qi = np.arange(WINDOW)[:, None]
    kj = np.arange(2 * WINDOW)[None, :]
    dist = np.maximum(WINDOW + qi - kj, 0)
    max_exact = REL_BUCKETS // 2
    scaled = np.log(np.maximum(dist, 1).astype(np.float32) / np.float32(max_exact)) / np.float32(
        math.log(REL_MAX_DIST / max_exact))
    large = max_exact + (scaled.astype(np.float32) * np.float32(REL_BUCKETS - max_exact)).astype(np.int32)
    large = np.minimum(large, REL_BUCKETS - 1)
    return np.where(dist < max_exact, dist, large).astype(np.int32)


def _bias_body(rb_ref, bucket_ref, o_ref):
    h = pl.program_id(0)
    bucket = bucket_ref[...]
    acc = jnp.zeros(bucket.shape, F32)
    for v in range(REL_BUCKETS):
        acc = jnp.where(bucket == v, rb_ref[v, h], acc)
    o_ref[0] = acc


def _rel_bias_table(rel_bias):
    bucket = jnp.asarray(_t5_bucket_band())
    return pl.pallas_call(
        _bias_body,
        grid=(ATTN_Q_HEADS,),
        in_specs=[
            pl.BlockSpec(memory_space=pltpu.SMEM),
            pl.BlockSpec((WINDOW, 2 * WINDOW), lambda h: (0, 0)),
        ],
        out_specs=pl.BlockSpec((1, WINDOW, 2 * WINDOW), lambda h: (h, 0, 0)),
        out_shape=jax.ShapeDtypeStruct((ATTN_Q_HEADS, WINDOW, 2 * WINDOW), F32),
        compiler_params=pltpu.CompilerParams(dimension_semantics=("arbitrary",)),
        name="rel_bias_table",
    )(rel_bias.astype(F32), bucket)


_IN_GROUPS = (
    ("q", ATTN_Q_DIM, BF16),
    ("k", ATTN_KV_DIM, BF16),
    ("v", ATTN_KV_DIM, BF16),
    ("dqkv", 3 * DN_DIM, BF16),
    ("dz", DN_DIM, BF16),
    ("ga", None, BF16),
    ("gd", None, BF16),
    ("bd", BD_PAD, F32),
)


def _in_groups(d_model):
    return tuple((n, d_model if w is None else w, dt) for n, w, dt in _IN_GROUPS)


def _pack_w_in(w_in):
    d = w_in.shape[0]
    o = 0
    aq = w_in[:, o:o + ATTN_Q_DIM] * (ATTN_HEAD_DIM ** -0.5); o += ATTN_Q_DIM
    ak = w_in[:, o:o + ATTN_KV_DIM]; o += ATTN_KV_DIM
    av = w_in[:, o:o + ATTN_KV_DIM]; o += ATTN_KV_DIM
    dqkv = w_in[:, o:o + 3 * DN_DIM]; o += 3 * DN_DIM
    dz = w_in[:, o:o + DN_DIM]; o += DN_DIM
    bd = w_in[:, o:o + 2 * DN_HEADS]; o += 2 * DN_HEADS
    ga = w_in[:, o:o + d]; o += d
    gd = w_in[:, o:o + d]; o += d
    bd = jnp.pad(bd, ((0, 0), (0, BD_PAD - 2 * DN_HEADS)))
    return jnp.concatenate([aq, ak, av, dqkv, dz, ga, gd, bd], axis=1).astype(BF16)


def _dn_activate(y, lo, cw_ref, xpad_ref, o_ref):
    rows, width = y.shape
    sub = V7X_SUBLANES
    cols = slice(lo, lo + width)
    xpad_ref[sub:sub + rows, cols] = y
    acc = cw_ref[DN_CONV - 1:DN_CONV, cols] * y
    for j in range(DN_CONV - 1):
        start = sub - (DN_CONV - 1) + j
        acc = acc + cw_ref[j:j + 1, cols] * xpad_ref[start:start + rows, cols]
    xpad_ref[0:sub, cols] = y[rows - sub:rows, :]
    a = _silu(acc)
    if lo < 2 * DN_DIM:
        heads = []
        for h in range(width // DN_HEAD_DIM):
            t = a[:, h * DN_HEAD_DIM:(h + 1) * DN_HEAD_DIM]
            t = t * lax.rsqrt(jnp.sum(t * t, axis=-1, keepdims=True) + L2_EPS)
            heads.append(t * (DN_HEAD_DIM ** -0.5) if lo < DN_DIM else t)
        a = jnp.concatenate(heads, axis=1)
    o_ref[0, :, cols] = a.astype(o_ref.dtype)


def _inproj_body(groups, x_ref, sc_ref, sh_ref, nw_ref, w_ref, cw_ref, *refs):
    out_refs, xpad_ref = refs[:-1], refs[-1]

    @pl.when(pl.program_id(1) == 0)
    def _():
        xpad_ref[0:V7X_SUBLANES, :] = jnp.zeros((V7X_SUBLANES, xpad_ref.shape[1]), F32)

    xf = x_ref[0]
    u = (_rms(xf, nw_ref[...]) * (1.0 + sc_ref[0]) + sh_ref[0]).astype(BF16)
    offs = np.cumsum([0] + [w for _, w, _ in groups])
    order = sorted(range(len(groups)), key=lambda i: groups[i][0] != "dqkv")
    for i in order:
        (name, width, dt), o_ref, off = groups[i], out_refs[i], int(offs[i])
        if name == "dqkv":
            for lo in range(0, width, DN_ACT_COLS):
                _dn_activate(_dot(u, w_ref[:, off + lo:off + lo + DN_ACT_COLS]), lo, cw_ref, xpad_ref, o_ref)
        else:
            o_ref[0] = _dot(u, w_ref[:, off:off + width]).astype(dt)


def _in_projection(x, sc1, sh1, norm_w, w_packed, dn_conv_w):
    bsz, seq, d = x.shape
    groups = _in_groups(d)
    n_tot = sum(w for _, w, _ in groups)
    tm = min(IN_ROW_TILE, seq)
    row = lambda width: pl.BlockSpec((1, tm, width), lambda b, t: (b, t, 0))
    per_batch = pl.BlockSpec((1, 1, d), lambda b, t: (b, 0, 0))
    out_bytes = sum(2 * tm * w * jnp.dtype(dt).itemsize for _, w, dt in groups)
    est = 2 * tm * d * 4 + d * n_tot * 2 + out_bytes + 4 * tm * d * 4 + 6 * tm * 3 * DN_DIM * 4
    return pl.pallas_call(
        functools.partial(_inproj_body, groups),
        grid=(bsz, seq // tm),
        in_specs=[row(d), per_batch, per_batch, _resident((1, d)), _resident((d, n_tot)),
                  _resident((DN_CONV, 3 * DN_DIM))],
        out_specs=[row(w) for _, w, _ in groups],
        out_shape=[jax.ShapeDtypeStruct((bsz, seq, w), dt) for _, w, dt in groups],
        scratch_shapes=[pltpu.VMEM((tm + V7X_SUBLANES, 3 * DN_DIM), F32)],
        compiler_params=pltpu.CompilerParams(
            dimension_semantics=("arbitrary", "arbitrary"), vmem_limit_bytes=_vmem_limit(est)),
        name="in_projection",
    )(x, sc1, sh1, norm_w.reshape(1, d), w_packed, dn_conv_w.astype(F32))


def _attn_body(sink_ref, q_ref, kp_ref, kc_ref, vp_ref, vc_ref, bias_ref, o_ref):
    n = pl.program_id(1)
    n_q = q_ref.shape[1] // WINDOW
    k_all = jnp.concatenate([kp_ref[0], kc_ref[0]], axis=0)
    v_all = jnp.concatenate([vp_ref[0], vc_ref[0]], axis=0)
    qi = lax.broadcasted_iota(jnp.int32, (WINDOW, 2 * WINDOW), 0)
    kj = lax.broadcasted_iota(jnp.int32, (WINDOW, 2 * WINDOW), 1)
    dist = WINDOW + qi - kj
    in_band = (dist >= 0) & (dist < WINDOW)
    first_valid = in_band & ((kj >= WINDOW) | (n > 0))
    units = [(j, h) for j in range(n_q) for h in range(ATTN_Q_HEADS)]
    head_cols = lambda x, i: x[:, i * ATTN_HEAD_DIM:(i + 1) * ATTN_HEAD_DIM]
    band = lambda x, j: x[j * WINDOW:(j + 2) * WINDOW]
    scores = [jnp.where(first_valid if j == 0 else in_band,
                        _dot_nt(head_cols(q_ref[0, j * WINDOW:(j + 1) * WINDOW, :], h),
                                head_cols(band(k_all, j), h // ATTN_GROUP)) + bias_ref[h], NEG_INF)
              for j, h in units]
    maxes = [jnp.maximum(jnp.max(s, axis=-1, keepdims=True), sink_ref[h]) for s, (_, h) in zip(scores, units)]
    probs = [jnp.exp(s - m) for s, m in zip(scores, maxes)]
    denoms = [jnp.sum(p, axis=-1, keepdims=True) + jnp.exp(sink_ref[h] - m)
              for p, m, (_, h) in zip(probs, maxes, units)]
    outs = [_dot(p.astype(BF16), head_cols(band(v_all, j), h // ATTN_GROUP)) / d
            for p, d, (j, h) in zip(probs, denoms, units)]
    for j in range(n_q):
        o_ref[0, j * WINDOW:(j + 1) * WINDOW, :] = jnp.concatenate(
            outs[j * ATTN_Q_HEADS:(j + 1) * ATTN_Q_HEADS], axis=1).astype(o_ref.dtype)


def _attention(q, k, v, sinks, bias):
    bsz, seq, _ = q.shape
    n_q = math.gcd(ATTN_BLOCKS, seq // WINDOW)
    cur = lambda width: pl.BlockSpec((1, n_q * WINDOW, width), lambda b, n: (b, n, 0))
    prev = lambda width: pl.BlockSpec((1, WINDOW, width), lambda b, n: (b, jnp.maximum(n * n_q - 1, 0), 0))
    return pl.pallas_call(
        _attn_body,
        grid=(bsz, seq // (n_q * WINDOW)),
        in_specs=[
            pl.BlockSpec(memory_space=pltpu.SMEM),
            cur(ATTN_Q_DIM), prev(ATTN_KV_DIM), cur(ATTN_KV_DIM), prev(ATTN_KV_DIM), cur(ATTN_KV_DIM),
            _resident((ATTN_Q_HEADS, WINDOW, 2 * WINDOW)),
        ],
        out_specs=cur(ATTN_Q_DIM),
        out_shape=jax.ShapeDtypeStruct((bsz, seq, ATTN_Q_DIM), BF16),
        compiler_params=pltpu.CompilerParams(dimension_semantics=("arbitrary", "arbitrary")),
        name="swa_attention",
    )(sinks.astype(F32), q, k, k, v, v, bias)


def _split3(a):
    hi = a.astype(BF16)
    r = a - hi.astype(F32)
    mid = r.astype(BF16)
    lo = (r - mid.astype(F32)).astype(BF16)
    return hi, mid, lo


def _mm(a, b, dot=_dot):
    return dot(a.astype(BF16), b.astype(BF16))


def _unit_lower_inverses(lowers, ii, jj):
    size = lowers[0].shape[0]
    eye = jnp.where(ii == jj, 1.0, 0.0)
    invs = [eye] * len(lowers)
    s = 1
    while s < size:
        sh = int(math.log2(s))
        joins = (jnp.right_shift(ii, sh + 1) == jnp.right_shift(jj, sh + 1)) & (
            jnp.right_shift(ii, sh) != jnp.right_shift(jj, sh))
        es = [jnp.where(joins, lower, 0.0) for lower in lowers]
        if s == 1:
            invs = [inv - e for inv, e in zip(invs, es)]
        else:
            tmps = [_mm(e, inv) for e, inv in zip(es, invs)]
            invs = [inv - _mm(inv, tmp) for inv, tmp in zip(invs, tmps)]
        s *= 2
    return invs


def _dn_body(qkv_ref, z_ref, bd_ref, alog_ref, dtb_ref, nw_ref, o_ref, state_ref):
    n_b, rows = qkv_ref.shape[0], qkv_ref.shape[1]
    hd = DN_HEAD_DIM

    @pl.when(pl.program_id(1) == 0)
    def _():
        state_ref[...] = jnp.zeros_like(state_ref)

    cc = DN_CHUNK
    n_c = rows // cc
    ri = lax.broadcasted_iota(jnp.int32, (rows, rows), 0)
    ci = lax.broadcasted_iota(jnp.int32, (rows, rows), 1)
    same_chunk = jnp.right_shift(ri, int(math.log2(cc))) == jnp.right_shift(ci, int(math.log2(cc)))
    tri = jnp.where(same_chunk & (ci <= ri), 1.0, 0.0).astype(BF16)
    ii = lax.broadcasted_iota(jnp.int32, (cc, cc), 0)
    jj = lax.broadcasted_iota(jnp.int32, (cc, cc), 1)
    incl = jj <= ii
    strict = jj < ii
    nw = nw_ref[...]

    q_l, k_l, rhs_l, gcc_l, eg_l, gend_l, lower_l, intra_l = [], [], [], [], [], [], [], []
    for b in range(n_b):
        bd = bd_ref[b]
        beta_all = _sigmoid(bd)
        g_all = -jnp.exp(alog_ref[...]) * jax.nn.softplus(bd + dtb_ref[...])
        g_hi, g_mid, g_lo = _split3(g_all)
        gc = _dot(tri, g_hi) + _dot(tri, g_mid) + _dot(tri, g_lo)
        gc_t = gc.T
        eg_all = jnp.exp(gc)
        for h in range(DN_HEADS):
            q_all = qkv_ref[b, :, h * hd:(h + 1) * hd].astype(F32)
            k_all = qkv_ref[b, :, DN_DIM + h * hd:DN_DIM + (h + 1) * hd].astype(F32)
            v_all = qkv_ref[b, :, 2 * DN_DIM + h * hd:2 * DN_DIM + (h + 1) * hd].astype(F32)
            col = DN_HEADS + h
            for c in range(n_c):
                r0 = c * cc
                q = q_all[r0:r0 + cc]
                k = k_all[r0:r0 + cc]
                beta = beta_all[r0:r0 + cc, h:h + 1]
                gcc = gc[r0:r0 + cc, col:col + 1]
                gcr = gc_t[col:col + 1, r0:r0 + cc]
                eg = eg_all[r0:r0 + cc, col:col + 1]
                decay = jnp.where(incl, jnp.exp(jnp.where(incl, gcc - gcr, 0.0)), 0.0)
                kb = k * beta
                q_l.append(q); k_l.append(k); gcc_l.append(gcc); eg_l.append(eg)
                gend_l.append(gc[r0 + cc - 1:r0 + cc, col:col + 1])
                rhs_l.append(jnp.concatenate([v_all[r0:r0 + cc] * beta, kb * eg], axis=1))
                lower_l.append(jnp.where(strict, _mm(kb, k, _dot_nt) * decay, 0.0))
                intra_l.append(jnp.where(incl, _mm(q, k, _dot_nt) * decay, 0.0))
    inv_l = _unit_lower_inverses(lower_l, ii, jj)
    sol_l = [_mm(inv, rhs) for inv, rhs in zip(inv_l, rhs_l)]

    pairs = [(b, h) for b in range(n_b) for h in range(DN_HEADS)]
    unit = lambda b, h, c: (b * DN_HEADS + h) * n_c + c
    state_l = [state_ref[b, h] for b, h in pairs]
    for c in range(n_c):
        r0 = c * cc
        us = [unit(b, h, c) for b, h in pairs]
        vnew_l = [sol_l[u][:, :hd] - _mm(sol_l[u][:, hd:], state) for u, state in zip(us, state_l)]
        o_l = [_mm(q_l[u] * eg_l[u], state) + _mm(intra_l[u], vnew)
               for u, state, vnew in zip(us, state_l, vnew_l)]
        state_l = [state * jnp.exp(gend_l[u]) + _mm(k_l[u] * jnp.exp(gend_l[u] - gcc_l[u]), vnew, _dot_tn)
                   for u, state, vnew in zip(us, state_l, vnew_l)]
        for (b, h), o in zip(pairs, o_l):
            y = _rms(o, nw) * _silu(z_ref[b, r0:r0 + cc, h * hd:(h + 1) * hd].astype(F32))
            o_ref[b, r0:r0 + cc, h * hd:(h + 1) * hd] = y.astype(o_ref.dtype)
    for (b, h), state in zip(pairs, state_l):
        state_ref[b, h] = state


def _deltanet(qkv_act, dz, bd, a_log, dt_bias, norm_w):
    bsz, seq, _ = qkv_act.shape
    rows = min(DN_ROWS, seq)
    n_b = math.gcd(DN_BATCH, bsz)
    lane_row = lambda vec: jnp.pad(vec.astype(F32), (DN_HEADS, BD_PAD - 2 * DN_HEADS)).reshape(1, BD_PAD)
    blk = lambda width: pl.BlockSpec((n_b, rows, width), lambda b, s: (b, s, 0))
    est = n_b * (2 * rows * (3 * DN_DIM * 2 + DN_DIM * 2 + BD_PAD * 4 + DN_DIM * 2) + 6 * rows * 3 * DN_DIM * 4
                 + DN_HEADS * 24 * rows * DN_CHUNK * 4)
    return pl.pallas_call(
        _dn_body,
        grid=(bsz // n_b, seq // rows),
        in_specs=[
            blk(3 * DN_DIM), blk(DN_DIM), blk(BD_PAD),
            _resident((1, BD_PAD)), _resident((1, BD_PAD)), _resident((1, DN_HEAD_DIM)),
        ],
        out_specs=blk(DN_DIM),
        out_shape=jax.ShapeDtypeStruct((bsz, seq, DN_DIM), BF16),
        scratch_shapes=[pltpu.VMEM((n_b, DN_HEADS, DN_HEAD_DIM, DN_HEAD_DIM), F32)],
        compiler_params=pltpu.CompilerParams(
            dimension_semantics=("arbitrary", "arbitrary"), vmem_limit_bytes=_vmem_limit(est)),
        name="gated_deltanet",
    )(qkv_act, dz, bd, lane_row(a_log), lane_row(dt_bias), norm_w.astype(F32).reshape(1, -1))


def _mix_body(x_ref, ya_ref, yd_ref, ga_ref, gd_ref, wab_ref, wdb_ref, wo_ref, npost_ref, g1_ref, npre_ref,
              sc2_ref, sh2_ref, h_ref, u_ref):
    tm = x_ref.shape[1]
    half = tm // SPLIT_ROWS
    groups = [slice(r0, r0 + half) for r0 in range(0, tm, half)]
    branch = [(_dot(ya_ref[0, rs, :], wab_ref[...]), _dot(yd_ref[0, rs, :], wdb_ref[...])) for rs in groups]
    ys = []
    for rs, (ba, bdn) in zip(groups, branch):
        merged = (_sigmoid(ga_ref[0, rs, :].astype(F32)) * ba
                  + _sigmoid(gd_ref[0, rs, :].astype(F32)) * bdn)
        ys.append(_dot(merged.astype(BF16), wo_ref[...]))
    for rs, y in zip(groups, ys):
        h1 = x_ref[0, rs, :] + g1_ref[0] * _rms(y, npost_ref[...])
        h_ref[0, rs, :] = h1
        u_ref[0, rs, :] = (_rms(h1, npre_ref[...]) * (1.0 + sc2_ref[0]) + sh2_ref[0]).astype(u_ref.dtype)


def _mix_out(x, ya, yd, ga, gd, w_ab, w_db, w_out, norm_post, g1, norm_pre, sc2, sh2):
    bsz, seq, d = x.shape
    tm = min(ROW_TILE, seq)
    row = lambda width: pl.BlockSpec((1, tm, width), lambda b, t: (b, t, 0))
    per_batch = pl.BlockSpec((1, 1, d), lambda b, t: (b, 0, 0))
    est = (2 * tm * d * (4 + 2 + 2 + 4 + 2) + 2 * tm * 2 * ATTN_Q_DIM * 2 + (2 * ATTN_Q_DIM + d) * d * 2
           + 6 * tm * d * 4)
    return pl.pallas_call(
        _mix_body,
        grid=(bsz, seq // tm),
        in_specs=[
            row(d), row(ATTN_Q_DIM), row(DN_DIM), row(d), row(d),
            _resident((ATTN_Q_DIM, d)), _resident((DN_DIM, d)), _resident((d, d)),
            _resident((1, d)), per_batch, _resident((1, d)), per_batch, per_batch,
        ],
        out_specs=[row(d), row(d)],
        out_shape=[jax.ShapeDtypeStruct((bsz, seq, d), F32), jax.ShapeDtypeStruct((bsz, seq, d), BF16)],
        compiler_params=pltpu.CompilerParams(
            dimension_semantics=("arbitrary", "arbitrary"), vmem_limit_bytes=_vmem_limit(est)),
        name="merge_out_projection",
    )(x, ya, yd, ga, gd, w_ab, w_db, w_out, norm_post.reshape(1, d), g1, norm_pre.reshape(1, d), sc2, sh2)


def _ffn_body(u_ref, h_ref, wup_ref, cw_ref, wdn_ref, npost_ref, g2_ref, o_ref, gpad_ref, vpad_ref, gcar_ref,
              vcar_ref, act_ref):
    tm = u_ref.shape[1]
    n_ch = D_FF // FF_CHUNK
    sub = V7X_SUBLANES

    @pl.when(pl.program_id(1) == 0)
    def _():
        gcar_ref[...] = jnp.zeros_like(gcar_ref)
        vcar_ref[...] = jnp.zeros_like(vcar_ref)

    u = u_ref[0]

    def conv_half(pad_ref, car_ref, j, off, scale):
        up = _dot(u, wup_ref[:, off:off + FF_CHUNK])
        pad_ref[0:sub, :] = car_ref[j]
        pad_ref[sub:sub + tm, :] = up
        car_ref[j] = up[tm - sub:tm, :]
        taps = cw_ref[:, off:off + FF_CHUNK] * scale
        out = taps[FFN_CONV - 1:FFN_CONV, :] * up
        for t in range(FFN_CONV - 1):
            start = sub - (FFN_CONV - 1) + t
            out = out + taps[t:t + 1, :] * pad_ref[start:start + tm, :]
        return out

    c0 = math.sqrt(2.0 / math.pi)
    for j in range(n_ch):
        gate = conv_half(gpad_ref, gcar_ref, j, j * FF_CHUNK, 1.0)
        half_val = conv_half(vpad_ref, vcar_ref, j, D_FF + j * FF_CHUNK, 0.5)
        t = jnp.tanh(gate * (c0 + (0.044715 * c0) * (gate * gate)))
        act_ref[:, j * FF_CHUNK:(j + 1) * FF_CHUNK] = ((gate + gate * t) * half_val).astype(BF16)

    y = _dot(act_ref[...], wdn_ref[...])
    o_ref[0] = h_ref[0] + g2_ref[0] * _rms(y, npost_ref[...])


def _ffn(u2, h1, w_up, conv_w, w_down, norm_post, g2):
    bsz, seq, d = h1.shape
    tm = min(ROW_TILE, seq)
    n_ch = D_FF // FF_CHUNK
    row = lambda width: pl.BlockSpec((1, tm, width), lambda b, t: (b, t, 0))
    per_batch = pl.BlockSpec((1, 1, d), lambda b, t: (b, 0, 0))
    est = (2 * tm * d * (2 + 4 + 4) + 3 * d * D_FF * 2 + 2 * (tm + 8) * FF_CHUNK * 4 + tm * D_FF * 2
           + 8 * tm * FF_CHUNK * 4 + 2 * tm * d * 4)
    return pl.pallas_call(
        _ffn_body,
        grid=(bsz, seq // tm),
        in_specs=[
            row(d), row(d),
            _resident((d, 2 * D_FF)), _resident((FFN_CONV, 2 * D_FF)), _resident((D_FF, d)),
            _resident((1, d)), per_batch,
        ],
        out_specs=row(d),
        out_shape=jax.ShapeDtypeStruct((bsz, seq, d), F32),
        scratch_shapes=[
            pltpu.VMEM((tm + V7X_SUBLANES, FF_CHUNK), F32),
            pltpu.VMEM((tm + V7X_SUBLANES, FF_CHUNK), F32),
            pltpu.VMEM((n_ch, V7X_SUBLANES, FF_CHUNK), F32),
            pltpu.VMEM((n_ch, V7X_SUBLANES, FF_CHUNK), F32),
            pltpu.VMEM((tm, D_FF), BF16),
        ],
        compiler_params=pltpu.CompilerParams(
            dimension_semantics=("arbitrary", "arbitrary"), vmem_limit_bytes=_vmem_limit(est)),
        name="conv_geglu_ffn",
    )(u2, h1, w_up, conv_w.astype(F32), w_down, norm_post.reshape(1, d), g2)


def _layer(h, c, ada_w, ada_b, norm_mix_pre, norm_mix_post, norm_ffn_pre, norm_ffn_post, w_in, dn_conv_w,
           dn_a_log, dn_dt_bias, dn_norm_w, attn_sinks, bias, w_attn_branch, w_dn_branch, w_out, ffn_w_up,
           ffn_conv_w, ffn_w_down):
    bsz, seq, d = h.shape
    mod = _modulation(c, ada_w, ada_b)
    sh1, sc1, g1, sh2, sc2, g2 = [mod[:, i * d:(i + 1) * d].reshape(bsz, 1, d) for i in range(N_MOD)]
    q, k, v, dqkv, dz, ga, gd, bd = _in_projection(h, sc1, sh1, norm_mix_pre, _pack_w_in(w_in), dn_conv_w)
    y_attn = _attention(q, k, v, attn_sinks, bias)
    y_dn = _deltanet(dqkv, dz, bd, dn_a_log, dn_dt_bias, dn_norm_w)
    h1, u2 = _mix_out(h, y_attn, y_dn, ga, gd, w_attn_branch.astype(BF16), w_dn_branch.astype(BF16),
                      w_out.astype(BF16), norm_mix_post, g1, norm_ffn_pre, sc2, sh2)
    return _ffn(u2, h1, ffn_w_up.astype(BF16), ffn_conv_w, ffn_w_down.astype(BF16), norm_ffn_post, g2)


def kernel(x, c, ada_w, ada_b, norm_mix_pre, norm_mix_post, norm_ffn_pre, norm_ffn_post, w_in, dn_conv_w, dn_a_log, dn_dt_bias, dn_norm_w, attn_sinks, rel_bias, w_attn_branch, w_dn_branch, w_out, ffn_w_up, ffn_conv_w, ffn_w_down):
    bias = _rel_bias_table(rel_bias)
    h = x
    for l in range(ada_w.shape[0]):
        h = _layer(h, c, ada_w[l], ada_b[l], norm_mix_pre[l], norm_mix_post[l], norm_ffn_pre[l],
                   norm_ffn_post[l], w_in[l], dn_conv_w[l], dn_a_log[l], dn_dt_bias[l], dn_norm_w[l],
                   attn_sinks[l], bias, w_attn_branch[l], w_dn_branch[l], w_out[l], ffn_w_up[l],
                   ffn_conv_w[l], ffn_w_down[l])
    return h
```

```python
import functools
import math

import numpy as np
import jax
import jax.numpy as jnp
from jax import lax
from jax.experimental import pallas as pl
from jax.experimental.pallas import tpu as pltpu

ATTN_Q_HEADS = 8
ATTN_KV_HEADS = 2
ATTN_HEAD_DIM = 64
WINDOW = 128
REL_BUCKETS = 32
REL_MAX_DIST = 128
DN_HEADS = 4
DN_HEAD_DIM = 128
DN_CONV = 4
DN_CHUNK = 128
D_FF = 2816
FFN_CONV = 3
RMS_EPS = 1e-6
L2_EPS = 1e-6
N_MOD = 6
NEG_INF = -1e30

ATTN_Q_DIM = ATTN_Q_HEADS * ATTN_HEAD_DIM
ATTN_KV_DIM = ATTN_KV_HEADS * ATTN_HEAD_DIM
DN_DIM = DN_HEADS * DN_HEAD_DIM
ATTN_GROUP = ATTN_Q_HEADS // ATTN_KV_HEADS

V7X_LANES = 128
V7X_SUBLANES = 8
V7X_VMEM_BYTES = 64 * 1024 * 1024

F32 = jnp.float32
BF16 = jnp.bfloat16

ROW_TILE = 1024
IN_ROW_TILE = 512
DN_ACT_COLS = 256
SPLIT_ROWS = 4
ATTN_BLOCKS = 16
DN_ROWS = 256
DN_BATCH = 2
FF_CHUNK = 256
BD_PAD = V7X_LANES


def _vmem_limit(nbytes):
    return int(min(V7X_VMEM_BYTES, nbytes + nbytes // 2 + (8 << 20)))


def _resident(shape):
    nd = len(shape)
    return pl.BlockSpec(shape, lambda *_: (0,) * nd, pipeline_mode=pl.Buffered(1))


def _dot(a, b):
    return jnp.dot(a, b, preferred_element_type=F32)


def _dot_nt(a, b):
    return lax.dot_general(a, b, (((1,), (1,)), ((), ())), preferred_element_type=F32)


def _dot_tn(a, b):
    return lax.dot_general(a, b, (((0,), (0,)), ((), ())), preferred_element_type=F32)


def _rms(xf, w):
    return xf * lax.rsqrt(jnp.mean(xf * xf, axis=-1, keepdims=True) + RMS_EPS) * w


def _sigmoid(x):
    return 1.0 / (1.0 + jnp.exp(-x))


def _silu(x):
    return x * _sigmoid(x)


def _mod_body(c_ref, w_ref, b_ref, o_ref):
    ca = _silu(c_ref[...])
    o_ref[...] = _dot(ca.astype(BF16), w_ref[...].astype(BF16)) + b_ref[...]


def _modulation(c, ada_w, ada_b):
    bsz, d = c.shape
    n = ada_w.shape[1]
    tn = d
    return pl.pallas_call(
        _mod_body,
        grid=(n // tn,),
        in_specs=[
            pl.BlockSpec((bsz, d), lambda j: (0, 0)),
            pl.BlockSpec((d, tn), lambda j: (0, j)),
            pl.BlockSpec((1, tn), lambda j: (0, j)),
        ],
        out_specs=pl.BlockSpec((bsz, tn), lambda j: (0, j)),
        out_shape=jax.ShapeDtypeStruct((bsz, n), F32),
        compiler_params=pltpu.CompilerParams(
            dimension_semantics=("arbitrary",), vmem_limit_bytes=_vmem_limit(2 * d * tn * 4)),
        name="adaln_modulation",
    )(c, ada_w, ada_b.reshape(1, n))


def _t5_bucket_band():
    qi = np.arange(WINDOW)[:, None]
    kj = np.arange(2 * WINDOW)[None, :]
    dist = np.maximum(WINDOW + qi - kj, 0)
    max_exact = REL_BUCKETS // 2
    scaled = np.log(np.maximum(dist, 1).astype(np.float32) / np.float32(max_exact)) / np.float32(
        math.log(REL_MAX_DIST / max_exact))
    large = max_exact + (scaled.astype(np.float32) * np.float32(REL_BUCKETS - max_exact)).astype(np.int32)
    large = np.minimum(large, REL_BUCKETS - 1)
    return np.where(dist < max_exact, dist, large).astype(np.int32)


def _bias_body(rb_ref, bucket_ref, o_ref):
    h = pl.program_id(0)
    bucket = bucket_ref[...]
    acc = jnp.zeros(bucket.shape, F32)
    for v in range(REL_BUCKETS):
        acc = jnp.where(bucket == v, rb_ref[v, h], acc)
    o_ref[0] = acc


def _rel_bias_table(rel_bias):
    bucket = jnp.asarray(_t5_bucket_band())
    return pl.pallas_call(
        _bias_body,
        grid=(ATTN_Q_HEADS,),
        in_specs=[
            pl.BlockSpec(memory_space=pltpu.SMEM),
            pl.BlockSpec((WINDOW, 2 * WINDOW), lambda h: (0, 0)),
        ],
        out_specs=pl.BlockSpec((1, WINDOW, 2 * WINDOW), lambda h: (h, 0, 0)),
        out_shape=jax.ShapeDtypeStruct((ATTN_Q_HEADS, WINDOW, 2 * WINDOW), F32),
        compiler_params=pltpu.CompilerParams(dimension_semantics=("arbitrary",)),
        name="rel_bias_table",
    )(rel_bias.astype(F32), bucket)


_IN_GROUPS = (
    ("q", ATTN_Q_DIM, BF16),
    ("k", ATTN_KV_DIM, BF16),
    ("v", ATTN_KV_DIM, BF16),
    ("dqkv", 3 * DN_DIM, BF16),
    ("dz", DN_DIM, BF16),
    ("ga", None, BF16),
    ("gd", None, BF16),
    ("bd", BD_PAD, F32),
)


def _in_groups(d_model):
    return tuple((n, d_model if w is None else w, dt) for n, w, dt in _IN_GROUPS)


def _pack_w_in(w_in):
    d = w_in.shape[0]
    wt = w_in.T
    o = 0
    aq = wt[o:o + ATTN_Q_DIM] * (ATTN_HEAD_DIM ** -0.5); o += ATTN_Q_DIM
    ak = wt[o:o + ATTN_KV_DIM]; o += ATTN_KV_DIM
    av = wt[o:o + ATTN_KV_DIM]; o += ATTN_KV_DIM
    dqkv = wt[o:o + 3 * DN_DIM]; o += 3 * DN_DIM
    dz = wt[o:o + DN_DIM]; o += DN_DIM
    bd = wt[o:o + 2 * DN_HEADS]; o += 2 * DN_HEADS
    ga = wt[o:o + d]; o += d
    gd = wt[o:o + d]; o += d
    bd = jnp.pad(bd, ((0, BD_PAD - 2 * DN_HEADS), (0, 0)))
    return jnp.concatenate([aq, ak, av, dqkv, dz, ga, gd, bd], axis=0).astype(BF16)


def _dn_activate(y, lo, cw_ref, xpad_ref, o_ref):
    rows, width = y.shape
    sub = V7X_SUBLANES
    cols = slice(lo, lo + width)
    xpad_ref[sub:sub + rows, cols] = y
    acc = cw_ref[DN_CONV - 1:DN_CONV, cols] * y
    for j in range(DN_CONV - 1):
        start = sub - (DN_CONV - 1) + j
        acc = acc + cw_ref[j:j + 1, cols] * xpad_ref[start:start + rows, cols]
    xpad_ref[0:sub, cols] = y[rows - sub:rows, :]
    a = _silu(acc)
    if lo < 2 * DN_DIM:
        heads = []
        for h in range(width // DN_HEAD_DIM):
            t = a[:, h * DN_HEAD_DIM:(h + 1) * DN_HEAD_DIM]
            t = t * lax.rsqrt(jnp.sum(t * t, axis=-1, keepdims=True) + L2_EPS)
            heads.append(t * (DN_HEAD_DIM ** -0.5) if lo < DN_DIM else t)
        a = jnp.concatenate(heads, axis=1)
    o_ref[0, :, cols] = a.astype(o_ref.dtype)


def _inproj_body(groups, x_ref, sc_ref, sh_ref, nw_ref, w_ref, cw_ref, *refs):
    out_refs, xpad_ref = refs[:-1], refs[-1]

    @pl.when(pl.program_id(1) == 0)
    def _():
        xpad_ref[0:V7X_SUBLANES, :] = jnp.zeros((V7X_SUBLANES, xpad_ref.shape[1]), F32)

    xf = x_ref[0]
    u = (_rms(xf, nw_ref[...]) * (1.0 + sc_ref[0]) + sh_ref[0]).astype(BF16)
    offs = np.cumsum([0] + [w for _, w, _ in groups])
    order = sorted(range(len(groups)), key=lambda i: groups[i][0] != "dqkv")
    for i in order:
        (name, width, dt), o_ref, off = groups[i], out_refs[i], int(offs[i])
        if name == "dqkv":
            for lo in range(0, width, DN_ACT_COLS):
                _dn_activate(_dot_nt(u, w_ref[off + lo:off + lo + DN_ACT_COLS, :]), lo, cw_ref, xpad_ref, o_ref)
        else:
            o_ref[0] = _dot_nt(u, w_ref[off:off + width, :]).astype(dt)


def _in_projection(x, sc1, sh1, norm_w, w_packed, dn_conv_w):
    bsz, seq, d = x.shape
    groups = _in_groups(d)
    n_tot = sum(w for _, w, _ in groups)
    tm = min(IN_ROW_TILE, seq)
    row = lambda width: pl.BlockSpec((1, tm, width), lambda b, t: (b, t, 0))
    per_batch = pl.BlockSpec((1, 1, d), lambda b, t: (b, 0, 0))
    out_bytes = sum(2 * tm * w * jnp.dtype(dt).itemsize for _, w, dt in groups)
    est = 2 * tm * d * 4 + d * n_tot * 2 + out_bytes + 4 * tm * d * 4 + 6 * tm * 3 * DN_DIM * 4
    return pl.pallas_call(
        functools.partial(_inproj_body, groups),
        grid=(bsz, seq // tm),
        in_specs=[row(d), per_batch, per_batch, _resident((1, d)), _resident((n_tot, d)),
                  _resident((DN_CONV, 3 * DN_DIM))],
        out_specs=[row(w) for _, w, _ in groups],
        out_shape=[jax.ShapeDtypeStruct((bsz, seq, w), dt) for _, w, dt in groups],
        scratch_shapes=[pltpu.VMEM((tm + V7X_SUBLANES, 3 * DN_DIM), F32)],
        compiler_params=pltpu.CompilerParams(
            dimension_semantics=("arbitrary", "arbitrary"), vmem_limit_bytes=_vmem_limit(est)),
        name="in_projection",
    )(x, sc1, sh1, norm_w.reshape(1, d), w_packed, dn_conv_w.astype(F32))


def _attn_body(sink_ref, q_ref, kp_ref, kc_ref, vp_ref, vc_ref, bias_ref, o_ref):
    n = pl.program_id(1)
    n_q = q_ref.shape[1] // WINDOW
    k_all = jnp.concatenate([kp_ref[0], kc_ref[0]], axis=0)
    v_all = jnp.concatenate([vp_ref[0], vc_ref[0]], axis=0)
    qi = lax.broadcasted_iota(jnp.int32, (WINDOW, 2 * WINDOW), 0)
    kj = lax.broadcasted_iota(jnp.int32, (WINDOW, 2 * WINDOW), 1)
    dist = WINDOW + qi - kj
    in_band = (dist >= 0) & (dist < WINDOW)
    first_valid = in_band & ((kj >= WINDOW) | (n > 0))
    units = [(j, h) for j in range(n_q) for h in range(ATTN_Q_HEADS)]
    head_cols = lambda x, i: x[:, i * ATTN_HEAD_DIM:(i + 1) * ATTN_HEAD_DIM]
    band = lambda x, j: x[j * WINDOW:(j + 2) * WINDOW]
    scores = [jnp.where(first_valid if j == 0 else in_band,
                        _dot_nt(head_cols(q_ref[0, j * WINDOW:(j + 1) * WINDOW, :], h),
                                head_cols(band(k_all, j), h // ATTN_GROUP)) + bias_ref[h], NEG_INF)
              for j, h in units]
    maxes = [jnp.maximum(jnp.max(s, axis=-1, keepdims=True), sink_ref[h]) for s, (_, h) in zip(scores, units)]
    probs = [jnp.exp(s - m) for s, m in zip(scores, maxes)]
    denoms = [jnp.sum(p, axis=-1, keepdims=True) + jnp.exp(sink_ref[h] - m)
              for p, m, (_, h) in zip(probs, maxes, units)]
    outs = [_dot(p.astype(BF16), head_cols(band(v_all, j), h // ATTN_GROUP)) / d
            for p, d, (j, h) in zip(probs, denoms, units)]
    for j in range(n_q):
        o_ref[0, j * WINDOW:(j + 1) * WINDOW, :] = jnp.concatenate(
            outs[j * ATTN_Q_HEADS:(j + 1) * ATTN_Q_HEADS], axis=1).astype(o_ref.dtype)


def _attention(q, k, v, sinks, bias):
    bsz, seq, _ = q.shape
    n_q = math.gcd(ATTN_BLOCKS, seq // WINDOW)
    cur = lambda width: pl.BlockSpec((1, n_q * WINDOW, width), lambda b, n: (b, n, 0))
    prev = lambda width: pl.BlockSpec((1, WINDOW, width), lambda b, n: (b, jnp.maximum(n * n_q - 1, 0), 0))
    return pl.pallas_call(
        _attn_body,
        grid=(bsz, seq // (n_q * WINDOW)),
        in_specs=[
            pl.BlockSpec(memory_space=pltpu.SMEM),
            cur(ATTN_Q_DIM), prev(ATTN_KV_DIM), cur(ATTN_KV_DIM), prev(ATTN_KV_DIM), cur(ATTN_KV_DIM),
            _resident((ATTN_Q_HEADS, WINDOW, 2 * WINDOW)),
        ],
        out_specs=cur(ATTN_Q_DIM),
        out_shape=jax.ShapeDtypeStruct((bsz, seq, ATTN_Q_DIM), BF16),
        compiler_params=pltpu.CompilerParams(dimension_semantics=("arbitrary", "arbitrary")),
        name="swa_attention",
    )(sinks.astype(F32), q, k, k, v, v, bias)


def _split3(a):
    hi = a.astype(BF16)
    r = a - hi.astype(F32)
    mid = r.astype(BF16)
    lo = (r - mid.astype(F32)).astype(BF16)
    return hi, mid, lo


def _mm(a, b, dot=_dot):
    return dot(a.astype(BF16), b.astype(BF16))


def _unit_lower_inverses(lowers, ii, jj):
    size = lowers[0].shape[0]
    eye = jnp.where(ii == jj, 1.0, 0.0)
    invs = [eye] * len(lowers)
    s = 1
    while s < size:
        sh = int(math.log2(s))
        joins = (jnp.right_shift(ii, sh + 1) == jnp.right_shift(jj, sh + 1)) & (
            jnp.right_shift(ii, sh) != jnp.right_shift(jj, sh))
        es = [jnp.where(joins, lower, 0.0) for lower in lowers]
        if s == 1:
            invs = [inv - e for inv, e in zip(invs, es)]
        else:
            tmps = [_mm(e, inv) for e, inv in zip(es, invs)]
            invs = [inv - _mm(inv, tmp) for inv, tmp in zip(invs, tmps)]
        s *= 2
    return invs


def _dn_body(qkv_ref, z_ref, bd_ref, alog_ref, dtb_ref, nw_ref, o_ref, state_ref):
    n_b, rows = qkv_ref.shape[0], qkv_ref.shape[1]
    hd = DN_HEAD_DIM

    @pl.when(pl.program_id(1) == 0)
    def _():
        state_ref[...] = jnp.zeros_like(state_ref)

    cc = DN_CHUNK
    n_c = rows // cc
    ri = lax.broadcasted_iota(jnp.int32, (rows, rows), 0)
    ci = lax.broadcasted_iota(jnp.int32, (rows, rows), 1)
    same_chunk = jnp.right_shift(ri, int(math.log2(cc))) == jnp.right_shift(ci, int(math.log2(cc)))
    tri = jnp.where(same_chunk & (ci <= ri), 1.0, 0.0).astype(BF16)
    ii = lax.broadcasted_iota(jnp.int32, (cc, cc), 0)
    jj = lax.broadcasted_iota(jnp.int32, (cc, cc), 1)
    incl = jj <= ii
    strict = jj < ii
    nw = nw_ref[...]

    q_l, k_l, rhs_l, gcc_l, eg_l, gend_l, lower_l, intra_l = [], [], [], [], [], [], [], []
    for b in range(n_b):
        bd = bd_ref[b]
        beta_all = _sigmoid(bd)
        g_all = -jnp.exp(alog_ref[...]) * jax.nn.softplus(bd + dtb_ref[...])
        g_hi, g_mid, g_lo = _split3(g_all)
        gc = _dot(tri, g_hi) + _dot(tri, g_mid) + _dot(tri, g_lo)
        gc_t = gc.T
        eg_all = jnp.exp(gc)
        for h in range(DN_HEADS):
            q_all = qkv_ref[b, :, h * hd:(h + 1) * hd].astype(F32)
            k_all = qkv_ref[b, :, DN_DIM + h * hd:DN_DIM + (h + 1) * hd].astype(F32)
            v_all = qkv_ref[b, :, 2 * DN_DIM + h * hd:2 * DN_DIM + (h + 1) * hd].astype(F32)
            col = DN_HEADS + h
            for c in range(n_c):
                r0 = c * cc
                q = q_all[r0:r0 + cc]
                k = k_all[r0:r0 + cc]
                beta = beta_all[r0:r0 + cc, h:h + 1]
                gcc = gc[r0:r0 + cc, col:col + 1]
                gcr = gc_t[col:col + 1, r0:r0 + cc]
                eg = eg_all[r0:r0 + cc, col:col + 1]
                decay = jnp.where(incl, jnp.exp(jnp.where(incl, gcc - gcr, 0.0)), 0.0)
                kb = k * beta
                q_l.append(q); k_l.append(k); gcc_l.append(gcc); eg_l.append(eg)
                gend_l.append(gc[r0 + cc - 1:r0 + cc, col:col + 1])
                rhs_l.append(jnp.concatenate([v_all[r0:r0 + cc] * beta, kb * eg], axis=1))
                lower_l.append(jnp.where(strict, _mm(kb, k, _dot_nt) * decay, 0.0))
                intra_l.append(jnp.where(incl, _mm(q, k, _dot_nt) * decay, 0.0))
    inv_l = _unit_lower_inverses(lower_l, ii, jj)
    sol_l = [_mm(inv, rhs) for inv, rhs in zip(inv_l, rhs_l)]

    pairs = [(b, h) for b in range(n_b) for h in range(DN_HEADS)]
    unit = lambda b, h, c: (b * DN_HEADS + h) * n_c + c
    state_l = [state_ref[b, h] for b, h in pairs]
    for c in range(n_c):
        r0 = c * cc
        us = [unit(b, h, c) for b, h in pairs]
        vnew_l = [sol_l[u][:, :hd] - _mm(sol_l[u][:, hd:], state) for u, state in zip(us, state_l)]
        o_l = [_mm(q_l[u] * eg_l[u], state) + _mm(intra_l[u], vnew)
               for u, state, vnew in zip(us, state_l, vnew_l)]
        state_l = [state * jnp.exp(gend_l[u]) + _mm(k_l[u] * jnp.exp(gend_l[u] - gcc_l[u]), vnew, _dot_tn)
                   for u, state, vnew in zip(us, state_l, vnew_l)]
        for (b, h), o in zip(pairs, o_l):
            y = _rms(o, nw) * _silu(z_ref[b, r0:r0 + cc, h * hd:(h + 1) * hd].astype(F32))
            o_ref[b, r0:r0 + cc, h * hd:(h + 1) * hd] = y.astype(o_ref.dtype)
    for (b, h), state in zip(pairs, state_l):
        state_ref[b, h] = state


def _deltanet(qkv_act, dz, bd, a_log, dt_bias, norm_w):
    bsz, seq, _ = qkv_act.shape
    rows = min(DN_ROWS, seq)
    n_b = math.gcd(DN_BATCH, bsz)
    lane_row = lambda vec: jnp.pad(vec.astype(F32), (DN_HEADS, BD_PAD - 2 * DN_HEADS)).reshape(1, BD_PAD)
    blk = lambda width: pl.BlockSpec((n_b, rows, width), lambda b, s: (b, s, 0))
    est = n_b * (2 * rows * (3 * DN_DIM * 2 + DN_DIM * 2 + BD_PAD * 4 + DN_DIM * 2) + 6 * rows * 3 * DN_DIM * 4
                 + DN_HEADS * 24 * rows * DN_CHUNK * 4)
    return pl.pallas_call(
        _dn_body,
        grid=(bsz // n_b, seq // rows),
        in_specs=[
            blk(3 * DN_DIM), blk(DN_DIM), blk(BD_PAD),
            _resident((1, BD_PAD)), _resident((1, BD_PAD)), _resident((1, DN_HEAD_DIM)),
        ],
        out_specs=blk(DN_DIM),
        out_shape=jax.ShapeDtypeStruct((bsz, seq, DN_DIM), BF16),
        scratch_shapes=[pltpu.VMEM((n_b, DN_HEADS, DN_HEAD_DIM, DN_HEAD_DIM), F32)],
        compiler_params=pltpu.CompilerParams(
            dimension_semantics=("arbitrary", "arbitrary"), vmem_limit_bytes=_vmem_limit(est)),
        name="gated_deltanet",
    )(qkv_act, dz, bd, lane_row(a_log), lane_row(dt_bias), norm_w.astype(F32).reshape(1, -1))


def _mix_body(x_ref, ya_ref, yd_ref, ga_ref, gd_ref, wab_ref, wdb_ref, wo_ref, npost_ref, g1_ref, npre_ref,
              sc2_ref, sh2_ref, h_ref, u_ref):
    tm = x_ref.shape[1]
    half = tm // SPLIT_ROWS
    groups = [slice(r0, r0 + half) for r0 in range(0, tm, half)]
    branch = [(_dot(ya_ref[0, rs, :], wab_ref[...]), _dot(yd_ref[0, rs, :], wdb_ref[...])) for rs in groups]
    ys = []
    for rs, (ba, bdn) in zip(groups, branch):
        merged = (_sigmoid(ga_ref[0, rs, :].astype(F32)) * ba
                  + _sigmoid(gd_ref[0, rs, :].astype(F32)) * bdn)
        ys.append(_dot(merged.astype(BF16), wo_ref[...]))
    for rs, y in zip(groups, ys):
        h1 = x_ref[0, rs, :] + g1_ref[0] * _rms(y, npost_ref[...])
        h_ref[0, rs, :] = h1
        u_ref[0, rs, :] = (_rms(h1, npre_ref[...]) * (1.0 + sc2_ref[0]) + sh2_ref[0]).astype(u_ref.dtype)


def _mix_out(x, ya, yd, ga, gd, w_ab, w_db, w_out, norm_post, g1, norm_pre, sc2, sh2):
    bsz, seq, d = x.shape
    tm = min(ROW_TILE, seq)
    row = lambda width: pl.BlockSpec((1, tm, width), lambda b, t: (b, t, 0))
    per_batch = pl.BlockSpec((1, 1, d), lambda b, t: (b, 0, 0))
    est = (2 * tm * d * (4 + 2 + 2 + 4 + 2) + 2 * tm * 2 * ATTN_Q_DIM * 2 + (2 * ATTN_Q_DIM + d) * d * 2
           + 6 * tm * d * 4)
    return pl.pallas_call(
        _mix_body,
        grid=(bsz, seq // tm),
        in_specs=[
            row(d), row(ATTN_Q_DIM), row(DN_DIM), row(d), row(d),
            _resident((ATTN_Q_DIM, d)), _resident((DN_DIM, d)), _resident((d, d)),
            _resident((1, d)), per_batch, _resident((1, d)), per_batch, per_batch,
        ],
        out_specs=[row(d), row(d)],
        out_shape=[jax.ShapeDtypeStruct((bsz, seq, d), F32), jax.ShapeDtypeStruct((bsz, seq, d), BF16)],
        compiler_params=pltpu.CompilerParams(
            dimension_semantics=("arbitrary", "arbitrary"), vmem_limit_bytes=_vmem_limit(est)),
        name="merge_out_projection",
    )(x, ya, yd, ga, gd, w_ab, w_db, w_out, norm_post.reshape(1, d), g1, norm_pre.reshape(1, d), sc2, sh2)


def _ffn_body(u_ref, h_ref, wup_ref, cw_ref, wdn_ref, npost_ref, g2_ref, o_ref, gpad_ref, vpad_ref, gcar_ref,
              vcar_ref, act_ref):
    tm = u_ref.shape[1]
    n_ch = D_FF // FF_CHUNK
    sub = V7X_SUBLANES

    @pl.when(pl.program_id(1) == 0)
    def _():
        gcar_ref[...] = jnp.zeros_like(gcar_ref)
        vcar_ref[...] = jnp.zeros_like(vcar_ref)

    u = u_ref[0]

    def conv_half(pad_ref, car_ref, j, off, scale):
        up = _dot(u, wup_ref[:, off:off + FF_CHUNK])
        pad_ref[0:sub, :] = car_ref[j]
        pad_ref[sub:sub + tm, :] = up
        car_ref[j] = up[tm - sub:tm, :]
        taps = cw_ref[:, off:off + FF_CHUNK] * scale
        out = taps[FFN_CONV - 1:FFN_CONV, :] * up
        for t in range(FFN_CONV - 1):
            start = sub - (FFN_CONV - 1) + t
            out = out + taps[t:t + 1, :] * pad_ref[start:start + tm, :]
        return out

    c0 = math.sqrt(2.0 / math.pi)
    for j in range(n_ch):
        gate = conv_half(gpad_ref, gcar_ref, j, j * FF_CHUNK, 1.0)
        half_val = conv_half(vpad_ref, vcar_ref, j, D_FF + j * FF_CHUNK, 0.5)
        t = jnp.tanh(gate * (c0 + (0.044715 * c0) * (gate * gate)))
        act_ref[:, j * FF_CHUNK:(j + 1) * FF_CHUNK] = ((gate + gate * t) * half_val).astype(BF16)

    y = _dot(act_ref[...], wdn_ref[...])
    o_ref[0] = h_ref[0] + g2_ref[0] * _rms(y, npost_ref[...])


def _ffn(u2, h1, w_up, conv_w, w_down, norm_post, g2):
    bsz, seq, d = h1.shape
    tm = min(ROW_TILE, seq)
    n_ch = D_FF // FF_CHUNK
    row = lambda width: pl.BlockSpec((1, tm, width), lambda b, t: (b, t, 0))
    per_batch = pl.BlockSpec((1, 1, d), lambda b, t: (b, 0, 0))
    est = (2 * tm * d * (2 + 4 + 4) + 3 * d * D_FF * 2 + 2 * (tm + 8) * FF_CHUNK * 4 + tm * D_FF * 2
           + 8 * tm * FF_CHUNK * 4 + 2 * tm * d * 4)
    return pl.pallas_call(
        _ffn_body,
        grid=(bsz, seq // tm),
        in_specs=[
            row(d), row(d),
            _resident((d, 2 * D_FF)), _resident((FFN_CONV, 2 * D_FF)), _resident((D_FF, d)),
            _resident((1, d)), per_batch,
        ],
        out_specs=row(d),
        out_shape=jax.ShapeDtypeStruct((bsz, seq, d), F32),
        scratch_shapes=[
            pltpu.VMEM((tm + V7X_SUBLANES, FF_CHUNK), F32),
            pltpu.VMEM((tm + V7X_SUBLANES, FF_CHUNK), F32),
            pltpu.VMEM((n_ch, V7X_SUBLANES, FF_CHUNK), F32),
            pltpu.VMEM((n_ch, V7X_SUBLANES, FF_CHUNK), F32),
            pltpu.VMEM((tm, D_FF), BF16),
        ],
        compiler_params=pltpu.CompilerParams(
            dimension_semantics=("arbitrary", "arbitrary"), vmem_limit_bytes=_vmem_limit(est)),
        name="conv_geglu_ffn",
    )(u2, h1, w_up, conv_w.astype(F32), w_down, norm_post.reshape(1, d), g2)


def _layer(h, c, ada_w, ada_b, norm_mix_pre, norm_mix_post, norm_ffn_pre, norm_ffn_post, w_in, dn_conv_w,
           dn_a_log, dn_dt_bias, dn_norm_w, attn_sinks, bias, w_attn_branch, w_dn_branch, w_out, ffn_w_up,
           ffn_conv_w, ffn_w_down):
    bsz, seq, d = h.shape
    mod = _modulation(c, ada_w, ada_b)
    sh1, sc1, g1, sh2, sc2, g2 = [mod[:, i * d:(i + 1) * d].reshape(bsz, 1, d) for i in range(N_MOD)]
    q, k, v, dqkv, dz, ga, gd, bd = _in_projection(h, sc1, sh1, norm_mix_pre, _pack_w_in(w_in), dn_conv_w)
    y_attn = _attention(q, k, v, attn_sinks, bias)
    y_dn = _deltanet(dqkv, dz, bd, dn_a_log, dn_dt_bias, dn_norm_w)
    h1, u2 = _mix_out(h, y_attn, y_dn, ga, gd, w_attn_branch.astype(BF16), w_dn_branch.astype(BF16),
                      w_out.astype(BF16), norm_mix_post, g1, norm_ffn_pre, sc2, sh2)
    return _ffn(u2, h1, ffn_w_up.astype(BF16), ffn_conv_w, ffn_w_down.astype(BF16), norm_ffn_post, g2)


def kernel(x, c, ada_w, ada_b, norm_mix_pre, norm_mix_post, norm_ffn_pre, norm_ffn_post, w_in, dn_conv_w, dn_a_log, dn_dt_bias, dn_norm_w, attn_sinks, rel_bias, w_attn_branch, w_dn_branch, w_out, ffn_w_up, ffn_conv_w, ffn_w_down):
    bias = _rel_bias_table(rel_bias)
    h = x
    for l in range(ada_w.shape[0]):
        h = _layer(h, c, ada_w[l], ada_b[l], norm_mix_pre[l], norm_mix_post[l], norm_ffn_pre[l],
                   norm_ffn_post[l], w_in[l], dn_conv_w[l], dn_a_log[l], dn_dt_bias[l], dn_norm_w[l],
                   attn_sinks[l], bias, w_attn_branch[l], w_dn_branch[l], w_out[l], ffn_w_up[l],
                   ffn_conv_w[l], ffn_w_down[l])
    return h
```

```python
import functools
import math

import numpy as np
import jax
import jax.numpy as jnp
from jax import lax
from jax.experimental import pallas as pl
from jax.experimental.pallas import tpu as pltpu

ATTN_Q_HEADS = 8
ATTN_KV_HEADS = 2
ATTN_HEAD_DIM = 64
WINDOW = 128
REL_BUCKETS = 32
REL_MAX_DIST = 128
DN_HEADS = 4
DN_HEAD_DIM = 128
DN_CONV = 4
DN_CHUNK = 128
D_FF = 2816
FFN_CONV = 3
RMS_EPS = 1e-6
L2_EPS = 1e-6
N_MOD = 6
NEG_INF = -1e30

ATTN_Q_DIM = ATTN_Q_HEADS * ATTN_HEAD_DIM
ATTN_KV_DIM = ATTN_KV_HEADS * ATTN_HEAD_DIM
DN_DIM = DN_HEADS * DN_HEAD_DIM
ATTN_GROUP = ATTN_Q_HEADS // ATTN_KV_HEADS

V7X_LANES = 128
V7X_SUBLANES = 8
V7X_VMEM_BYTES = 64 * 1024 * 1024

F32 = jnp.float32
BF16 = jnp.bfloat16

ROW_TILE = 1024
IN_ROW_TILE = 512
DN_ACT_COLS = 256
SPLIT_ROWS = 4
ATTN_BLOCKS = 16
DN_ROWS = 256
DN_BATCH = 2
FF_CHUNK = 256
BD_PAD = V7X_LANES


def _vmem_limit(nbytes):
    return int(min(V7X_VMEM_BYTES, nbytes + nbytes // 2 + (8 << 20)))


def _resident(shape):
    nd = len(shape)
    return pl.BlockSpec(shape, lambda *_: (0,) * nd, pipeline_mode=pl.Buffered(1))


def _dot(a, b):
    return jnp.dot(a, b, preferred_element_type=F32)


def _dot_nt(a, b):
    return lax.dot_general(a, b, (((1,), (1,)), ((), ())), preferred_element_type=F32)


def _dot_tn(a, b):
    return lax.dot_general(a, b, (((0,), (0,)), ((), ())), preferred_element_type=F32)


def _rms(xf, w):
    return xf * lax.rsqrt(jnp.mean(xf * xf, axis=-1, keepdims=True) + RMS_EPS) * w


def _sigmoid(x):
    return 1.0 / (1.0 + jnp.exp(-x))


def _silu(x):
    return x * _sigmoid(x)


def _mod_body(c_ref, w_ref, b_ref, o_ref):
    ca = _silu(c_ref[...])
    o_ref[...] = _dot(ca.astype(BF16), w_ref[...].astype(BF16)) + b_ref[...]


def _modulation(c, ada_w, ada_b):
    bsz, d = c.shape
    n = ada_w.shape[1]
    tn = d
    return pl.pallas_call(
        _mod_body,
        grid=(n // tn,),
        in_specs=[
            pl.BlockSpec((bsz, d), lambda j: (0, 0)),
            pl.BlockSpec((d, tn), lambda j: (0, j)),
            pl.BlockSpec((1, tn), lambda j: (0, j)),
        ],
        out_specs=pl.BlockSpec((bsz, tn), lambda j: (0, j)),
        out_shape=jax.ShapeDtypeStruct((bsz, n), F32),
        compiler_params=pltpu.CompilerParams(
            dimension_semantics=("arbitrary",), vmem_limit_bytes=_vmem_limit(2 * d * tn * 4)),
        name="adaln_modulation",
    )(c, ada_w, ada_b.reshape(1, n))


def _t5_bucket_band():
    qi = np.arange(WINDOW)[:, None]
    kj = np.arange(2 * WINDOW)[None, :]
    dist = np.maximum(WINDOW + qi - kj, 0)
    max_exact = REL_BUCKETS // 2
    scaled = np.log(np.maximum(dist, 1).astype(np.float32) / np.float32(max_exact)) / np.float32(
        math.log(REL_MAX_DIST / max_exact))
    large = max_exact + (scaled.astype(np.float32) * np.float32(REL_BUCKETS - max_exact)).astype(np.int32)
    large = np.minimum(large, REL_BUCKETS - 1)
    return np.where(dist < max_exact, dist, large).astype(np.int32)


def _bias_body(rb_ref, bucket_ref, o_ref):
    h = pl.program_id(0)
    bucket = bucket_ref[...]
    acc = jnp.zeros(bucket.shape, F32)
    for v in range(REL_BUCKETS):
        acc = jnp.where(bucket == v, rb_ref[v, h], acc)
    o_ref[0] = acc


def _rel_bias_table(rel_bias):
    bucket = jnp.asarray(_t5_bucket_band())
    return pl.pallas_call(
        _bias_body,
        grid=(ATTN_Q_HEADS,),
        in_specs=[
            pl.BlockSpec(memory_space=pltpu.SMEM),
            pl.BlockSpec((WINDOW, 2 * WINDOW), lambda h: (0, 0)),
        ],
        out_specs=pl.BlockSpec((1, WINDOW, 2 * WINDOW), lambda h: (h, 0, 0)),
        out_shape=jax.ShapeDtypeStruct((ATTN_Q_HEADS, WINDOW, 2 * WINDOW), F32),
        compiler_params=pltpu.CompilerParams(dimension_semantics=("arbitrary",)),
        name="rel_bias_table",
    )(rel_bias.astype(F32), bucket)


_IN_GROUPS = (
    ("q", ATTN_Q_DIM, BF16),
    ("k", ATTN_KV_DIM, BF16),
    ("v", ATTN_KV_DIM, BF16),
    ("dqkv", 3 * DN_DIM, BF16),
    ("dz", DN_DIM, BF16),
    ("ga", None, BF16),
    ("gd", None, BF16),
    ("bd", BD_PAD, F32),
)


def _in_groups(d_model):
    return tuple((n, d_model if w is None else w, dt) for n, w, dt in _IN_GROUPS)


def _pack_w_in(w_in):
    d = w_in.shape[0]
    o = 0
    aq = w_in[:, o:o + ATTN_Q_DIM] * (ATTN_HEAD_DIM ** -0.5); o += ATTN_Q_DIM
    ak = w_in[:, o:o + ATTN_KV_DIM]; o += ATTN_KV_DIM
    av = w_in[:, o:o + ATTN_KV_DIM]; o += ATTN_KV_DIM
    dqkv = w_in[:, o:o + 3 * DN_DIM]; o += 3 * DN_DIM
    dz = w_in[:, o:o + DN_DIM]; o += DN_DIM
    bd = w_in[:, o:o + 2 * DN_HEADS]; o += 2 * DN_HEADS
    ga = w_in[:, o:o + d]; o += d
    gd = w_in[:, o:o + d]; o += d
    bd = jnp.pad(bd, ((0, 0), (0, BD_PAD - 2 * DN_HEADS)))
    return jnp.concatenate([aq, ak, av, dqkv, dz, ga, gd, bd], axis=1).astype(BF16)


def _dn_activate(y, lo, cw_ref, xpad_ref, o_ref):
    rows, width = y.shape
    sub = V7X_SUBLANES
    cols = slice(lo, lo + width)
    xpad_ref[sub:sub + rows, cols] = y
    acc = cw_ref[DN_CONV - 1:DN_CONV, cols] * y
    for j in range(DN_CONV - 1):
        start = sub - (DN_CONV - 1) + j
        acc = acc + cw_ref[j:j + 1, cols] * xpad_ref[start:start + rows, cols]
    xpad_ref[0:sub, cols] = y[rows - sub:rows, :]
    a = _silu(acc)
    if lo < 2 * DN_DIM:
        heads = []
        for h in range(width // DN_HEAD_DIM):
            t = a[:, h * DN_HEAD_DIM:(h + 1) * DN_HEAD_DIM]
            t = t * lax.rsqrt(jnp.sum(t * t, axis=-1, keepdims=True) + L2_EPS)
            heads.append(t * (DN_HEAD_DIM ** -0.5) if lo < DN_DIM else t)
        a = jnp.concatenate(heads, axis=1)
    o_ref[0, :, cols] = a.astype(o_ref.dtype)


def _inproj_body(groups, x_ref, sc_ref, sh_ref, nw_ref, w_ref, cw_ref, *refs):
    out_refs, xpad_ref = refs[:-1], refs[-1]

    @pl.when(pl.program_id(1) == 0)
    def _():
        xpad_ref[0:V7X_SUBLANES, :] = jnp.zeros((V7X_SUBLANES, xpad_ref.shape[1]), F32)

    xf = x_ref[0]
    u = (_rms(xf, nw_ref[...]) * (1.0 + sc_ref[0]) + sh_ref[0]).astype(BF16)
    offs = np.cumsum([0] + [w for _, w, _ in groups])
    order = sorted(range(len(groups)), key=lambda i: groups[i][0] != "dqkv")
    for i in order:
        (name, width, dt), o_ref, off = groups[i], out_refs[i], int(offs[i])
        if name == "dqkv":
            for lo in range(0, width, DN_ACT_COLS):
                _dn_activate(_dot(u, w_ref[:, off + lo:off + lo + DN_ACT_COLS]), lo, cw_ref, xpad_ref, o_ref)
        else:
            o_ref[0] = _dot(u, w_ref[:, off:off + width]).astype(dt)


def _in_projection(x, sc1, sh1, norm_w, w_packed, dn_conv_w):
    bsz, seq, d = x.shape
    groups = _in_groups(d)
    n_tot = sum(w for _, w, _ in groups)
    tm = min(IN_ROW_TILE, seq)
    row = lambda width: pl.BlockSpec((1, tm, width), lambda b, t: (b, t, 0))
    per_batch = pl.BlockSpec((1, 1, d), lambda b, t: (b, 0, 0))
    out_bytes = sum(2 * tm * w * jnp.dtype(dt).itemsize for _, w, dt in groups)
    est = 2 * tm * d * 4 + d * n_tot * 2 + out_bytes + 4 * tm * d * 4 + 6 * tm * 3 * DN_DIM * 4
    return pl.pallas_call(
        functools.partial(_inproj_body, groups),
        grid=(bsz, seq // tm),
        in_specs=[row(d), per_batch, per_batch, _resident((1, d)), _resident((d, n_tot)),
                  _resident((DN_CONV, 3 * DN_DIM))],
        out_specs=[row(w) for _, w, _ in groups],
        out_shape=[jax.ShapeDtypeStruct((bsz, seq, w), dt) for _, w, dt in groups],
        scratch_shapes=[pltpu.VMEM((tm + V7X_SUBLANES, 3 * DN_DIM), F32)],
        compiler_params=pltpu.CompilerParams(
            dimension_semantics=("arbitrary", "arbitrary"), vmem_limit_bytes=_vmem_limit(est)),
        name="in_projection",
    )(x, sc1, sh1, norm_w.reshape(1, d), w_packed, dn_conv_w.astype(F32))


def _attn_body(sink_ref, q_ref, kp_ref, kc_ref, vp_ref, vc_ref, bias_ref, o_ref):
    n = pl.program_id(1)
    n_q = q_ref.shape[1] // WINDOW
    k_all = jnp.concatenate([kp_ref[0], kc_ref[0]], axis=0)
    v_all = jnp.concatenate([vp_ref[0], vc_ref[0]], axis=0)
    qi = lax.broadcasted_iota(jnp.int32, (WINDOW, 2 * WINDOW), 0)
    kj = lax.broadcasted_iota(jnp.int32, (WINDOW, 2 * WINDOW), 1)
    dist = WINDOW + qi - kj
    in_band = (dist >= 0) & (dist < WINDOW)
    first_valid = in_band & ((kj >= WINDOW) | (n > 0))
    units = [(j, h) for j in range(n_q) for h in range(ATTN_Q_HEADS)]
    head_cols = lambda x, i: x[:, i * ATTN_HEAD_DIM:(i + 1) * ATTN_HEAD_DIM]
    band = lambda x, j: x[j * WINDOW:(j + 2) * WINDOW]
    scores = [jnp.where(first_valid if j == 0 else in_band,
                        _dot_nt(head_cols(q_ref[0, j * WINDOW:(j + 1) * WINDOW, :], h),
                                head_cols(band(k_all, j), h // ATTN_GROUP)) + bias_ref[h], NEG_INF)
              for j, h in units]
    maxes = [jnp.maximum(jnp.max(s, axis=-1, keepdims=True), sink_ref[h]) for s, (_, h) in zip(scores, units)]
    probs = [jnp.exp(s - m) for s, m in zip(scores, maxes)]
    denoms = [jnp.sum(p, axis=-1, keepdims=True) + jnp.exp(sink_ref[h] - m)
              for p, m, (_, h) in zip(probs, maxes, units)]
    outs = [_dot(p.astype(BF16), head_cols(band(v_all, j), h // ATTN_GROUP)) / d
            for p, d, (j, h) in zip(probs, denoms, units)]
    for j in range(n_q):
        o_ref[0, j * WINDOW:(j + 1) * WINDOW, :] = jnp.concatenate(
            outs[j * ATTN_Q_HEADS:(j + 1) * ATTN_Q_HEADS], axis=1).astype(o_ref.dtype)


def _attention(q, k, v, sinks, bias):
    bsz, seq, _ = q.shape
    n_q = math.gcd(ATTN_BLOCKS, seq // WINDOW)
    cur = lambda width: pl.BlockSpec((1, n_q * WINDOW, width), lambda b, n: (b, n, 0))
    prev = lambda width: pl.BlockSpec((1, WINDOW, width), lambda b, n: (b, jnp.maximum(n * n_q - 1, 0), 0))
    return pl.pallas_call(
        _attn_body,
        grid=(bsz, seq // (n_q * WINDOW)),
        in_specs=[
            pl.BlockSpec(memory_space=pltpu.SMEM),
            cur(ATTN_Q_DIM), prev(ATTN_KV_DIM), cur(ATTN_KV_DIM), prev(ATTN_KV_DIM), cur(ATTN_KV_DIM),
            _resident((ATTN_Q_HEADS, WINDOW, 2 * WINDOW)),
        ],
        out_specs=cur(ATTN_Q_DIM),
        out_shape=jax.ShapeDtypeStruct((bsz, seq, ATTN_Q_DIM), BF16),
        compiler_params=pltpu.CompilerParams(dimension_semantics=("arbitrary", "arbitrary")),
        name="swa_attention",
    )(sinks.astype(F32), q, k, k, v, v, bias)


def _split3(a):
    hi = a.astype(BF16)
    r = a - hi.astype(F32)
    mid = r.astype(BF16)
    lo = (r - mid.astype(F32)).astype(BF16)
    return hi, mid, lo


def _mm(a, b, dot=_dot):
    return dot(a.astype(BF16), b.astype(BF16))


def _unit_lower_inverses(lowers, ii, jj):
    size = lowers[0].shape[0]
    eye = jnp.where(ii == jj, 1.0, 0.0)
    invs = [eye] * len(lowers)
    s = 1
    while s < size:
        sh = int(math.log2(s))
        joins = (jnp.right_shift(ii, sh + 1) == jnp.right_shift(jj, sh + 1)) & (
            jnp.right_shift(ii, sh) != jnp.right_shift(jj, sh))
        es = [jnp.where(joins, lower, 0.0) for lower in lowers]
        if s == 1:
            invs = [inv - e for inv, e in zip(invs, es)]
        else:
            tmps = [_mm(e, inv) for e, inv in zip(es, invs)]
            invs = [inv - _mm(inv, tmp) for inv, tmp in zip(invs, tmps)]
        s *= 2
    return invs


def _dn_body(qkv_ref, z_ref, bd_ref, alog_ref, dtb_ref, nw_ref, o_ref, state_ref):
    n_b, rows = qkv_ref.shape[0], qkv_ref.shape[1]
    hd = DN_HEAD_DIM

    @pl.when(pl.program_id(1) == 0)
    def _():
        state_ref[...] = jnp.zeros_like(state_ref)

    cc = DN_CHUNK
    n_c = rows // cc
    ri = lax.broadcasted_iota(jnp.int32, (rows, rows), 0)
    ci = lax.broadcasted_iota(jnp.int32, (rows, rows), 1)
    same_chunk = jnp.right_shift(ri, int(math.log2(cc))) == jnp.right_shift(ci, int(math.log2(cc)))
    tri = jnp.where(same_chunk & (ci <= ri), 1.0, 0.0).astype(BF16)
    ii = lax.broadcasted_iota(jnp.int32, (cc, cc), 0)
    jj = lax.broadcasted_iota(jnp.int32, (cc, cc), 1)
    incl = jj <= ii
    strict = jj < ii
    nw = nw_ref[...]

    q_l, k_l, rhs_l, gcc_l, eg_l, gend_l, lower_l, intra_l = [], [], [], [], [], [], [], []
    for b in range(n_b):
        bd = bd_ref[b]
        beta_all = _sigmoid(bd)
        g_all = -jnp.exp(alog_ref[...]) * jax.nn.softplus(bd + dtb_ref[...])
        g_hi, g_mid, g_lo = _split3(g_all)
        gc = _dot(tri, g_hi) + _dot(tri, g_mid) + _dot(tri, g_lo)
        gc_t = gc.T
        eg_all = jnp.exp(gc)
        beta_t = beta_all.T
        eg_t = jnp.exp(gc_t)
        for h in range(DN_HEADS):
            q_all = qkv_ref[b, :, h * hd:(h + 1) * hd].astype(F32)
            k_all = qkv_ref[b, :, DN_DIM + h * hd:DN_DIM + (h + 1) * hd].astype(F32)
            v_all = qkv_ref[b, :, 2 * DN_DIM + h * hd:2 * DN_DIM + (h + 1) * hd].astype(F32)
            col = DN_HEADS + h
            for c in range(n_c):
                r0 = c * cc
                q = q_all[r0:r0 + cc]
                k = k_all[r0:r0 + cc]
                beta = beta_all[r0:r0 + cc, h:h + 1]
                gcc = gc[r0:r0 + cc, col:col + 1]
                gcr = gc_t[col:col + 1, r0:r0 + cc]
                eg = eg_all[r0:r0 + cc, col:col + 1]
                decay = jnp.where(incl, jnp.exp(jnp.where(incl, gcc - gcr, 0.0)), 0.0)
                kb = k * beta
                q_l.append(q); k_l.append(k); gcc_l.append(gcc); eg_l.append(eg)
                gend_l.append(gc[r0 + cc - 1:r0 + cc, col:col + 1])
                beta_r = beta_t[h:h + 1, r0:r0 + cc]
                rhs_l.append((v_all[r0:r0 + cc], k, beta_r, beta_r * eg_t[col:col + 1, r0:r0 + cc]))
                lower_l.append(jnp.where(strict, _mm(kb, k, _dot_nt) * decay, 0.0))
                intra_l.append(jnp.where(incl, _mm(q, k, _dot_nt) * decay, 0.0))
    inv_l = _unit_lower_inverses(lower_l, ii, jj)
    u_l = [_mm(inv * beta_r, v) for inv, (v, _, beta_r, _) in zip(inv_l, rhs_l)]
    w_l = [_mm(inv * beg_r, k) for inv, (_, k, _, beg_r) in zip(inv_l, rhs_l)]

    pairs = [(b, h) for b in range(n_b) for h in range(DN_HEADS)]
    unit = lambda b, h, c: (b * DN_HEADS + h) * n_c + c
    state_l = [state_ref[b, h] for b, h in pairs]
    for c in range(n_c):
        r0 = c * cc
        us = [unit(b, h, c) for b, h in pairs]
        vnew_l = [u_l[u] - _mm(w_l[u], state) for u, state in zip(us, state_l)]
        o_l = [_mm(q_l[u] * eg_l[u], state) + _mm(intra_l[u], vnew)
               for u, state, vnew in zip(us, state_l, vnew_l)]
        state_l = [state * jnp.exp(gend_l[u]) + _mm(k_l[u] * jnp.exp(gend_l[u] - gcc_l[u]), vnew, _dot_tn)
                   for u, state, vnew in zip(us, state_l, vnew_l)]
        for (b, h), o in zip(pairs, o_l):
            y = _rms(o, nw) * _silu(z_ref[b, r0:r0 + cc, h * hd:(h + 1) * hd].astype(F32))
            o_ref[b, r0:r0 + cc, h * hd:(h + 1) * hd] = y.astype(o_ref.dtype)
    for (b, h), state in zip(pairs, state_l):
        state_ref[b, h] = state


def _deltanet(qkv_act, dz, bd, a_log, dt_bias, norm_w):
    bsz, seq, _ = qkv_act.shape
    rows = min(DN_ROWS, seq)
    n_b = math.gcd(DN_BATCH, bsz)
    lane_row = lambda vec: jnp.pad(vec.astype(F32), (DN_HEADS, BD_PAD - 2 * DN_HEADS)).reshape(1, BD_PAD)
    blk = lambda width: pl.BlockSpec((n_b, rows, width), lambda b, s: (b, s, 0))
    est = n_b * (2 * rows * (3 * DN_DIM * 2 + DN_DIM * 2 + BD_PAD * 4 + DN_DIM * 2) + 6 * rows * 3 * DN_DIM * 4
                 + DN_HEADS * 24 * rows * DN_CHUNK * 4)
    return pl.pallas_call(
        _dn_body,
        grid=(bsz // n_b, seq // rows),
        in_specs=[
            blk(3 * DN_DIM), blk(DN_DIM), blk(BD_PAD),
            _resident((1, BD_PAD)), _resident((1, BD_PAD)), _resident((1, DN_HEAD_DIM)),
        ],
        out_specs=blk(DN_DIM),
        out_shape=jax.ShapeDtypeStruct((bsz, seq, DN_DIM), BF16),
        scratch_shapes=[pltpu.VMEM((n_b, DN_HEADS, DN_HEAD_DIM, DN_HEAD_DIM), F32)],
        compiler_params=pltpu.CompilerParams(
            dimension_semantics=("arbitrary", "arbitrary"), vmem_limit_bytes=_vmem_limit(est)),
        name="gated_deltanet",
    )(qkv_act, dz, bd, lane_row(a_log), lane_row(dt_bias), norm_w.astype(F32).reshape(1, -1))


def _mix_body(x_ref, ya_ref, yd_ref, ga_ref, gd_ref, wab_ref, wdb_ref, wo_ref, npost_ref, g1_ref, npre_ref,
              sc2_ref, sh2_ref, h_ref, u_ref):
    tm = x_ref.shape[1]
    half = tm // SPLIT_ROWS
    groups = [slice(r0, r0 + half) for r0 in range(0, tm, half)]
    branch = [(_dot(ya_ref[0, rs, :], wab_ref[...]), _dot(yd_ref[0, rs, :], wdb_ref[...])) for rs in groups]
    ys = []
    for rs, (ba, bdn) in zip(groups, branch):
        merged = (_sigmoid(ga_ref[0, rs, :].astype(F32)) * ba
                  + _sigmoid(gd_ref[0, rs, :].astype(F32)) * bdn)
        ys.append(_dot(merged.astype(BF16), wo_ref[...]))
    for rs, y in zip(groups, ys):
        h1 = x_ref[0, rs, :] + g1_ref[0] * _rms(y, npost_ref[...])
        h_ref[0, rs, :] = h1
        u_ref[0, rs, :] = (_rms(h1, npre_ref[...]) * (1.0 + sc2_ref[0]) + sh2_ref[0]).astype(u_ref.dtype)


def _mix_out(x, ya, yd, ga, gd, w_ab, w_db, w_out, norm_post, g1, norm_pre, sc2, sh2):
    bsz, seq, d = x.shape
    tm = min(ROW_TILE, seq)
    row = lambda width: pl.BlockSpec((1, tm, width), lambda b, t: (b, t, 0))
    per_batch = pl.BlockSpec((1, 1, d), lambda b, t: (b, 0, 0))
    est = (2 * tm * d * (4 + 2 + 2 + 4 + 2) + 2 * tm * 2 * ATTN_Q_DIM * 2 + (2 * ATTN_Q_DIM + d) * d * 2
           + 6 * tm * d * 4)
    return pl.pallas_call(
        _mix_body,
        grid=(bsz, seq // tm),
        in_specs=[
            row(d), row(ATTN_Q_DIM), row(DN_DIM), row(d), row(d),
            _resident((ATTN_Q_DIM, d)), _resident((DN_DIM, d)), _resident((d, d)),
            _resident((1, d)), per_batch, _resident((1, d)), per_batch, per_batch,
        ],
        out_specs=[row(d), row(d)],
        out_shape=[jax.ShapeDtypeStruct((bsz, seq, d), F32), jax.ShapeDtypeStruct((bsz, seq, d), BF16)],
        compiler_params=pltpu.CompilerParams(
            dimension_semantics=("arbitrary", "arbitrary"), vmem_limit_bytes=_vmem_limit(est)),
        name="merge_out_projection",
    )(x, ya, yd, ga, gd, w_ab, w_db, w_out, norm_post.reshape(1, d), g1, norm_pre.reshape(1, d), sc2, sh2)


def _ffn_body(u_ref, h_ref, wup_ref, cw_ref, wdn_ref, npost_ref, g2_ref, o_ref, gpad_ref, vpad_ref, gcar_ref,
              vcar_ref, act_ref):
    tm = u_ref.shape[1]
    n_ch = D_FF // FF_CHUNK
    sub = V7X_SUBLANES

    @pl.when(pl.program_id(1) == 0)
    def _():
        gcar_ref[...] = jnp.zeros_like(gcar_ref)
        vcar_ref[...] = jnp.zeros_like(vcar_ref)

    u = u_ref[0]

    def conv_half(pad_ref, car_ref, j, off, scale):
        up = _dot(u, wup_ref[:, off:off + FF_CHUNK])
        pad_ref[0:sub, :] = car_ref[j]
        pad_ref[sub:sub + tm, :] = up
        car_ref[j] = up[tm - sub:tm, :]
        taps = cw_ref[:, off:off + FF_CHUNK] * scale
        out = taps[FFN_CONV - 1:FFN_CONV, :] * up
        for t in range(FFN_CONV - 1):
            start = sub - (FFN_CONV - 1) + t
            out = out + taps[t:t + 1, :] * pad_ref[start:start + tm, :]
        return out

    c0 = math.sqrt(2.0 / math.pi)
    for j in range(n_ch):
        gate = conv_half(gpad_ref, gcar_ref, j, j * FF_CHUNK, 1.0)
        half_val = conv_half(vpad_ref, vcar_ref, j, D_FF + j * FF_CHUNK, 0.5)
        t = jnp.tanh(gate * (c0 + (0.044715 * c0) * (gate * gate)))
        act_ref[:, j * FF_CHUNK:(j + 1) * FF_CHUNK] = ((gate + gate * t) * half_val).astype(BF16)

    y = _dot(act_ref[...], wdn_ref[...])
    o_ref[0] = h_ref[0] + g2_ref[0] * _rms(y, npost_ref[...])


def _ffn(u2, h1, w_up, conv_w, w_down, norm_post, g2):
    bsz, seq, d = h1.shape
    tm = min(ROW_TILE, seq)
    n_ch = D_FF // FF_CHUNK
    row = lambda width: pl.BlockSpec((1, tm, width), lambda b, t: (b, t, 0))
    per_batch = pl.BlockSpec((1, 1, d), lambda b, t: (b, 0, 0))
    est = (2 * tm * d * (2 + 4 + 4) + 3 * d * D_FF * 2 + 2 * (tm + 8) * FF_CHUNK * 4 + tm * D_FF * 2
           + 8 * tm * FF_CHUNK * 4 + 2 * tm * d * 4)
    return pl.pallas_call(
        _ffn_body,
        grid=(bsz, seq // tm),
        in_specs=[
            row(d), row(d),
            _resident((d, 2 * D_FF)), _resident((FFN_CONV, 2 * D_FF)), _resident((D_FF, d)),
            _resident((1, d)), per_batch,
        ],
        out_specs=row(d),
        out_shape=jax.ShapeDtypeStruct((bsz, seq, d), F32),
        scratch_shapes=[
            pltpu.VMEM((tm + V7X_SUBLANES, FF_CHUNK), F32),
            pltpu.VMEM((tm + V7X_SUBLANES, FF_CHUNK), F32),
            pltpu.VMEM((n_ch, V7X_SUBLANES, FF_CHUNK), F32),
            pltpu.VMEM((n_ch, V7X_SUBLANES, FF_CHUNK), F32),
            pltpu.VMEM((tm, D_FF), BF16),
        ],
        compiler_params=pltpu.CompilerParams(
            dimension_semantics=("arbitrary", "arbitrary"), vmem_limit_bytes=_vmem_limit(est)),
        name="conv_geglu_ffn",
    )(u2, h1, w_up, conv_w.astype(F32), w_down, norm_post.reshape(1, d), g2)


def _layer(h, c, ada_w, ada_b, norm_mix_pre, norm_mix_post, norm_ffn_pre, norm_ffn_post, w_in, dn_conv_w,
           dn_a_log, dn_dt_bias, dn_norm_w, attn_sinks, bias, w_attn_branch, w_dn_branch, w_out, ffn_w_up,
           ffn_conv_w, ffn_w_down):
    bsz, seq, d = h.shape
    mod = _modulation(c, ada_w, ada_b)
    sh1, sc1, g1, sh2, sc2, g2 = [mod[:, i * d:(i + 1) * d].reshape(bsz, 1, d) for i in range(N_MOD)]
    q, k, v, dqkv, dz, ga, gd, bd = _in_projection(h, sc1, sh1, norm_mix_pre, _pack_w_in(w_in), dn_conv_w)
    y_attn = _attention(q, k, v, attn_sinks, bias)
    y_dn = _deltanet(dqkv, dz, bd, dn_a_log, dn_dt_bias, dn_norm_w)
    h1, u2 = _mix_out(h, y_attn, y_dn, ga, gd, w_attn_branch.astype(BF16), w_dn_branch.astype(BF16),
                      w_out.astype(BF16), norm_mix_post, g1, norm_ffn_pre, sc2, sh2)
    return _ffn(u2, h1, ffn_w_up.astype(BF16), ffn_conv_w, ffn_w_down.astype(BF16), norm_ffn_post, g2)


def kernel(x, c, ada_w, ada_b, norm_mix_pre, norm_mix_post, norm_ffn_pre, norm_ffn_post, w_in, dn_conv_w, dn_a_log, dn_dt_bias, dn_norm_w, attn_sinks, rel_bias, w_attn_branch, w_dn_branch, w_out, ffn_w_up, ffn_conv_w, ffn_w_down):
    bias = _rel_bias_table(rel_bias)
    h = x
    for l in range(ada_w.shape[0]):
        h = _layer(h, c, ada_w[l], ada_b[l], norm_mix_pre[l], norm_mix_post[l], norm_ffn_pre[l],
                   norm_ffn_post[l], w_in[l], dn_conv_w[l], dn_a_log[l], dn_dt_bias[l], dn_norm_w[l],
                   attn_sinks[l], bias, w_attn_branch[l], w_dn_branch[l], w_out[l], ffn_w_up[l],
                   ffn_conv_w[l], ffn_w_down[l])
    return h
```
